```python
import math
import jax
import jax.numpy as jnp
from jax import lax
import numpy as np

D_MODEL = 2048
BATCH = 4
SEQ = 8192
DEPTH = 4

D_MIX = D_MODEL
MLA_HEADS = 8
QK_NOPE_DIM = 128
QK_ROPE_DIM = 64
V_HEAD_DIM = 128
Q_LORA_RANK = 512
KV_LORA_RANK = 512
ROPE_THETA = 10000.0
Q_BLOCK = 128
MLA_WIDTH = MLA_HEADS * V_HEAD_DIM
CONV_CH = 512
CONV_WIDTH = 31
HGRN_HEADS = 4
HGRN_DK = 128
HGRN_DV = 128
HGRN_WIDTH = HGRN_HEADS * HGRN_DV
HGRN_CHUNK = 64
D_FF = ((8 * D_MODEL // 3 + 255) // 256) * 256
RMS_EPS = 1e-6
LN_EPS = 1e-5

COLS_MLA = (Q_LORA_RANK, KV_LORA_RANK, QK_ROPE_DIM)
COLS_CONV = (2 * CONV_CH,)
COLS_HGRN = (HGRN_HEADS * HGRN_DK, HGRN_HEADS * HGRN_DK,
             HGRN_WIDTH, HGRN_WIDTH)
IN_COLS = sum(COLS_MLA) + sum(COLS_CONV) + sum(COLS_HGRN)
SPLIT_POINTS = tuple(int(v) for v in np.cumsum(COLS_MLA + COLS_CONV + COLS_HGRN)[:-1])

kernel_name = "hymba_style_mla_conformer_hgrn2_trunk"


def rms_norm(x, w, eps=RMS_EPS):
    xf = x.astype(jnp.float32)
    y = xf * lax.rsqrt(jnp.mean(xf * xf, axis=-1, keepdims=True) + eps)
    return (y * w.astype(jnp.float32)).astype(x.dtype)


def layer_norm(x, w, b, eps=LN_EPS):
    xf = x.astype(jnp.float32)
    mu = jnp.mean(xf, axis=-1, keepdims=True)
    var = jnp.mean(jnp.square(xf - mu), axis=-1, keepdims=True)
    y = (xf - mu) * lax.rsqrt(var + eps)
    return (y * w.astype(jnp.float32) + b.astype(jnp.float32)).astype(x.dtype)


def rope_tables(positions):
    inv_freq = ROPE_THETA ** (-jnp.arange(0, QK_ROPE_DIM, 2, dtype=jnp.float32) / QK_ROPE_DIM)
    ang = positions.astype(jnp.float32)[..., None] * inv_freq
    return jnp.cos(ang), jnp.sin(ang)


def apply_rope(x, cos, sin):
    x1, x2 = jnp.split(x, 2, axis=-1)
    out = jnp.concatenate([x1 * cos - x2 * sin, x2 * cos + x1 * sin], axis=-1)
    return out.astype(x.dtype)


def mla_group(c_q, c_kv, k_rope_raw, cos, sin, q_norm_w, w_uq, kv_norm_w, w_ukv, out_norm_w):
    B, S, _ = c_q.shape
    q = (rms_norm(c_q, q_norm_w) @ w_uq).reshape(B, S, MLA_HEADS, QK_NOPE_DIM + QK_ROPE_DIM)
    q_nope, q_pe = q[..., :QK_NOPE_DIM], q[..., QK_NOPE_DIM:]
    q_pe = apply_rope(q_pe, cos[:, :, None, :], sin[:, :, None, :])
    kv = (rms_norm(c_kv, kv_norm_w) @ w_ukv).reshape(B, S, MLA_HEADS, QK_NOPE_DIM + V_HEAD_DIM)
    k_nope, v = kv[..., :QK_NOPE_DIM], kv[..., QK_NOPE_DIM:]
    k_pe = apply_rope(k_rope_raw, cos, sin)
    scale = (QK_NOPE_DIM + QK_ROPE_DIM) ** -0.5
    nb = S // Q_BLOCK
    qn_b = q_nope.reshape(B, nb, Q_BLOCK, MLA_HEADS, QK_NOPE_DIM).transpose(1, 0, 3, 2, 4)
    qp_b = q_pe.reshape(B, nb, Q_BLOCK, MLA_HEADS, QK_ROPE_DIM).transpose(1, 0, 3, 2, 4)
    key_idx = jnp.arange(S)

    def attend(args):
        qn, qp, blk = args
        s = (jnp.einsum('bhqd,bkhd->bhqk', qn, k_nope)
             + jnp.einsum('bhqr,bkr->bhqk', qp, k_pe)).astype(jnp.float32) * scale
        q_idx = blk * Q_BLOCK + jnp.arange(Q_BLOCK)
        mask = key_idx[None, :] <= q_idx[:, None]
        p = jax.nn.softmax(jnp.where(mask, s, -jnp.inf), axis=-1).astype(v.dtype)
        return jnp.einsum('bhqk,bkhd->bqhd', p, v)

    o = lax.map(attend, (qn_b, qp_b, jnp.arange(nb)))
    o = o.transpose(1, 0, 2, 3, 4).reshape(B, S, MLA_HEADS, V_HEAD_DIM)
    o = rms_norm(o, out_norm_w)
    return o.reshape(B, S, MLA_WIDTH)


def conv_group(u, conv_w, conv_b, ln_w, ln_b):
    a, gate = jnp.split(u, 2, axis=-1)
    h = a * jax.nn.sigmoid(gate)
    h = lax.conv_general_dilated(
        h, conv_w[:, None, :].astype(h.dtype), window_strides=(1,),
        padding=[(CONV_WIDTH - 1, 0)],
        dimension_numbers=('NWC', 'WIO', 'NWC'),
        feature_group_count=CONV_CH) + conv_b
    h = layer_norm(h, ln_w, ln_b)
    return jax.nn.silu(h)


def hgrn2_group(q_raw, f_raw, i_raw, g_raw, lower_bound, norm_w):
    B, S, _ = q_raw.shape
    H, DK, DV, C = HGRN_HEADS, HGRN_DK, HGRN_DV, HGRN_CHUNK
    nc = S // C
    q = jax.nn.silu(q_raw.astype(jnp.float32)).reshape(B, S, H, DK)
    zf = f_raw.astype(jnp.float32).reshape(B, S, H, DK)
    lb = lower_bound.reshape(H, DK)
    log_f = jnp.logaddexp(jnp.log(lb), jnp.log1p(-lb) + jax.nn.log_sigmoid(zf))
    k = -jnp.expm1(log_f)
    v = i_raw.astype(jnp.float32).reshape(B, S, H, DV)

    def to_chunks(t):
        return t.reshape(B, nc, C, H, t.shape[-1]).transpose(1, 0, 3, 2, 4)

    causal = jnp.tril(jnp.ones((C, C), dtype=bool))

    def step(state, xs):
        qc, kc, vc, lf = xs
        b = jnp.cumsum(lf, axis=-2)
        inter = jnp.einsum('bhtk,bhkv->bhtv', qc * jnp.exp(b), state)
        diff = b[:, :, :, None, :] - b[:, :, None, :, :]
        decay = jnp.exp(jnp.where(causal[:, :, None], diff, -jnp.inf))
        scores = jnp.einsum('bhtk,bhsk,bhtsk->bhts', qc, kc, decay)
        intra = jnp.einsum('bhts,bhsv->bhtv', scores, vc)
        b_last = b[:, :, -1:, :]
        new_state = (jnp.exp(b_last)[:, :, 0, :, None] * state
                     + jnp.einsum('bhsk,bhsv->bhkv', kc * jnp.exp(b_last - b), vc))
        return new_state, inter + intra

    state0 = jnp.zeros((B, H, DK, DV), jnp.float32)
    _, o = lax.scan(step, state0, (to_chunks(q), to_chunks(k), to_chunks(v), to_chunks(log_f)))
    o = o.transpose(1, 0, 3, 2, 4).reshape(B, S, H, DV).astype(q_raw.dtype)
    g = jax.nn.silu(g_raw).reshape(B, S, H, DV)
    o = rms_norm(o, norm_w) * g
    return o.reshape(B, S, HGRN_WIDTH)


def setup_inputs(seed: int = 0) -> dict:
    key = jax.random.key(seed)
    ks = jax.random.split(key, 24)
    f32 = jnp.float32

    def nrm(k, shape, scale):
        return jax.random.normal(k, shape, f32) * scale

    def gain(k, shape):
        return 1.0 + 0.02 * jax.random.normal(k, shape, f32)

    x = jax.random.normal(ks[0], (BATCH, SEQ, D_MODEL), f32)
    start = jax.random.randint(ks[1], (BATCH, 1), 0, 4096, dtype=jnp.int32)
    positions = (start + jnp.arange(SEQ, dtype=jnp.int32)[None, :]).astype(jnp.int32)
    return {
        "x": x,
        "positions": positions,
        "attn_norm_w": gain(ks[2], (DEPTH, D_MODEL)),
        "w_in": nrm(ks[3], (DEPTH, D_MODEL, IN_COLS), D_MODEL ** -0.5),
        "q_norm_w": gain(ks[4], (DEPTH, Q_LORA_RANK)),
        "w_uq": nrm(ks[5], (DEPTH, Q_LORA_RANK, MLA_HEADS * (QK_NOPE_DIM + QK_ROPE_DIM)), Q_LORA_RANK ** -0.5),
        "kv_norm_w": gain(ks[6], (DEPTH, KV_LORA_RANK)),
        "w_ukv": nrm(ks[7], (DEPTH, KV_LORA_RANK, MLA_HEADS * (QK_NOPE_DIM + V_HEAD_DIM)), KV_LORA_RANK ** -0.5),
        "mla_out_norm_w": gain(ks[8], (DEPTH, V_HEAD_DIM)),
        "conv_w": nrm(ks[9], (DEPTH, CONV_WIDTH, CONV_CH), CONV_WIDTH ** -0.5),
        "conv_b": nrm(ks[10], (DEPTH, CONV_CH), 0.02),
        "conv_ln_w": gain(ks[11], (DEPTH, CONV_CH)),
        "conv_ln_b": nrm(ks[12], (DEPTH, CONV_CH), 0.02),
        "hgrn_lower_bounds": nrm(ks[13], (DEPTH, HGRN_HEADS * HGRN_DK), 0.1),
        "hgrn_norm_w": gain(ks[14], (DEPTH, HGRN_DV)),
        "w_out": nrm(ks[15], (DEPTH, D_MIX, D_MODEL), D_MIX ** -0.5),
        "ffn_norm_w": gain(ks[16], (DEPTH, D_MODEL)),
        "w_gate": nrm(ks[17], (DEPTH, D_MODEL, D_FF), D_MODEL ** -0.5),
        "w_up": nrm(ks[18], (DEPTH, D_MODEL, D_FF), D_MODEL ** -0.5),
        "w_down": nrm(ks[19], (DEPTH, D_FF, D_MODEL), D_FF ** -0.5),
        "final_norm_w": gain(ks[20], (D_MODEL,)),
    }


def reference(x, positions, attn_norm_w, w_in, q_norm_w, w_uq, kv_norm_w, w_ukv, mla_out_norm_w,
              conv_w, conv_b, conv_ln_w, conv_ln_b, hgrn_lower_bounds, hgrn_norm_w, w_out,
              ffn_norm_w, w_gate, w_up, w_down, final_norm_w):
    cos, sin = rope_tables(positions)
    lb_all = jax.nn.softmax(hgrn_lower_bounds.astype(jnp.float32), axis=0)
    lb_all = jnp.cumsum(lb_all, axis=0)
    lb_all = lb_all - lb_all[0:1]

    h = x
    for l in range(DEPTH):
        xn = rms_norm(h, attn_norm_w[l])
        proj = xn @ w_in[l]
        (c_q, c_kv, k_rope_raw, conv_in,
         hq, hf, hi, hg) = jnp.split(proj, SPLIT_POINTS, axis=-1)
        y_a = mla_group(c_q, c_kv, k_rope_raw, cos, sin, q_norm_w[l], w_uq[l],
                        kv_norm_w[l], w_ukv[l], mla_out_norm_w[l])
        y_b = conv_group(conv_in, conv_w[l], conv_b[l], conv_ln_w[l], conv_ln_b[l])
        y_c = hgrn2_group(hq, hf, hi, hg, lb_all[l], hgrn_norm_w[l])
        mix = jnp.concatenate([y_a, y_b, y_c], axis=-1)
        h = h + mix @ w_out[l]

        xn = rms_norm(h, ffn_norm_w[l])
        ff = jax.nn.silu(xn @ w_gate[l]) * (xn @ w_up[l])
        h = h + ff @ w_down[l]
    return rms_norm(h, final_norm_w)
```

```python
import functools
import math

import jax
import jax.numpy as jnp
import numpy as np
from jax import lax
from jax.experimental import pallas as pl
from jax.experimental.pallas import tpu as pltpu

F32 = jnp.float32
BF16 = jnp.bfloat16

MLA_HEADS = 8
QK_NOPE = 128
QK_ROPE = 64
V_HEAD = 128
Q_LORA = 512
KV_LORA = 512
ROPE_THETA = 10000.0
CONV_CH = 512
CONV_WIDTH = 31
HGRN_HEADS = 4
HGRN_DK = 128
HGRN_DV = 128
RMS_EPS = 1e-6
LN_EPS = 1e-5
QK_CAT = 256
MAIN_COLS = 4096
CONV_HALO = 32

TM_IN = 512
TN_IN = 512
TS_QKV = 512
TQ_ATT = 512
TS_CONV = 512
L_HGRN = 128
TM_OUT = 512
TN_OUT = 512
TM_UP = 512
TF_UP = 512
TM_DOWN = 512
TN_DOWN = 512
TM_FINAL = 512
MIB = 1024 * 1024


def _params(sem, vmem_mib):
    return pltpu.CompilerParams(dimension_semantics=sem, vmem_limit_bytes=vmem_mib * MIB)


def _rms_rows(x, w):
    ms = jnp.mean(x * x, axis=-1, keepdims=True)
    return x * lax.rsqrt(ms + RMS_EPS) * w


def _sigmoid(x):
    return 1.0 / (1.0 + jnp.exp(-x))


def _in_proj_body(nj, x_ref, nw_ref, wm_ref, wk_ref, om_ref, ok_ref, xn_ref):
    j = pl.program_id(1)

    @pl.when(j == 0)
    def _():
        xn_ref[...] = _rms_rows(x_ref[...], nw_ref[...]).astype(BF16)

    @pl.when(j < nj)
    def _():
        om_ref[...] = jnp.dot(xn_ref[...], wm_ref[...], preferred_element_type=F32).astype(BF16)

    @pl.when(j == nj)
    def _():
        ok_ref[...] = jnp.dot(xn_ref[...], wk_ref[...], preferred_element_type=F32)


def _in_proj(h, norm_w, w_main, w_kr, l):
    t, d = h.shape
    nj = MAIN_COLS // TN_IN
    return pl.pallas_call(
        functools.partial(_in_proj_body, nj),
        grid=(t // TM_IN, nj + 1),
        in_specs=[
            pl.BlockSpec((TM_IN, d), lambda i, j: (i, 0)),
            pl.BlockSpec((None, 1, d), lambda i, j: (l, 0, 0)),
            pl.BlockSpec((None, d, TN_IN), lambda i, j: (l, 0, jnp.minimum(j, nj - 1))),
            pl.BlockSpec((None, d, 128), lambda i, j: (l, 0, 0)),
        ],
        out_specs=[
            pl.BlockSpec((TM_IN, TN_IN), lambda i, j: (i, jnp.minimum(j, nj - 1))),
            pl.BlockSpec((TM_IN, 128), lambda i, j: (i, 0)),
        ],
        out_shape=[jax.ShapeDtypeStruct((t, MAIN_COLS), BF16), jax.ShapeDtypeStruct((t, 128), F32)],
        scratch_shapes=[pltpu.VMEM((TM_IN, d), BF16)],
        compiler_params=_params(("parallel", "arbitrary"), 40),
        name="in_proj",
    )(h, norm_w, w_main, w_kr)


def _q_proj_body(scale, c_ref, nw_ref, w_ref, cs_ref, o_ref):
    cn = _rms_rows(c_ref[...].astype(F32), nw_ref[...]).astype(BF16)
    q = jnp.dot(cn, w_ref[...], preferred_element_type=F32)
    cs = cs_ref[...] * scale
    for hd in range(MLA_HEADS):
        base = hd * QK_CAT
        o_ref[hd, :, 0:QK_NOPE] = (q[:, base:base + QK_NOPE] * scale).astype(BF16)
        o_ref[hd, :, QK_NOPE:QK_CAT] = (q[:, base + QK_NOPE:base + QK_CAT] * cs).astype(BF16)


def _q_proj(proj_main, norm_w, w_uq, cs, l, b, s):
    ns = s // TS_QKV
    scale = (QK_NOPE + QK_ROPE) ** -0.5
    return pl.pallas_call(
        functools.partial(_q_proj_body, scale),
        grid=(b, ns),
        in_specs=[
            pl.BlockSpec((TS_QKV, Q_LORA), lambda bi, i: (bi * ns + i, 0)),
            pl.BlockSpec((None, 1, Q_LORA), lambda bi, i: (l, 0, 0)),
            pl.BlockSpec((None, Q_LORA, MLA_HEADS * QK_CAT), lambda bi, i: (l, 0, 0)),
            pl.BlockSpec((TS_QKV, 128), lambda bi, i: (bi * ns + i, 0)),
        ],
        out_specs=pl.BlockSpec((None, MLA_HEADS, TS_QKV, QK_CAT), lambda bi, i: (bi, 0, i, 0)),
        out_shape=jax.ShapeDtypeStruct((b, MLA_HEADS, s, QK_CAT), BF16),
        compiler_params=_params(("parallel", "parallel"), 40),
        name="q_proj",
    )(proj_main, norm_w, w_uq, cs)


def _kv_proj_body(c_ref, kr_ref, nw_ref, w_ref, cs_ref, k_ref, v_ref):
    cn = _rms_rows(c_ref[...].astype(F32), nw_ref[...]).astype(BF16)
    kv = jnp.dot(cn, w_ref[...], preferred_element_type=F32)
    a = kr_ref[...] * cs_ref[...]
    krot = (a + pltpu.roll(a, 64, axis=1)).astype(BF16)
    for hd in range(MLA_HEADS):
        k_ref[hd, :, 0:QK_NOPE] = kv[:, hd * QK_NOPE:(hd + 1) * QK_NOPE].astype(BF16)
        k_ref[hd, :, QK_NOPE:QK_CAT] = krot
        v0 = MLA_HEADS * QK_NOPE + hd * V_HEAD
        v_ref[hd] = kv[:, v0:v0 + V_HEAD].astype(BF16)


def _kv_proj(proj_main, kr, norm_w, w_ukv, cs, l, b, s):
    ns = s // TS_QKV
    return pl.pallas_call(
        _kv_proj_body,
        grid=(b, ns),
        in_specs=[
            pl.BlockSpec((TS_QKV, KV_LORA), lambda bi, i: (bi * ns + i, 1)),
            pl.BlockSpec((TS_QKV, 128), lambda bi, i: (bi * ns + i, 0)),
            pl.BlockSpec((None, 1, KV_LORA), lambda bi, i: (l, 0, 0)),
            pl.BlockSpec((None, KV_LORA, MLA_HEADS * (QK_NOPE + V_HEAD)), lambda bi, i: (l, 0, 0)),
            pl.BlockSpec((TS_QKV, 128), lambda bi, i: (bi * ns + i, 0)),
        ],
        out_specs=[
            pl.BlockSpec((None, MLA_HEADS, TS_QKV, QK_CAT), lambda bi, i: (bi, 0, i, 0)),
            pl.BlockSpec((None, MLA_HEADS, TS_QKV, V_HEAD), lambda bi, i: (bi, 0, i, 0)),
        ],
        out_shape=[
            jax.ShapeDtypeStruct((b, MLA_HEADS, s, QK_CAT), BF16),
            jax.ShapeDtypeStruct((b, MLA_HEADS, s, V_HEAD), BF16),
        ],
        compiler_params=_params(("parallel", "parallel"), 40),
        name="kv_proj",
    )(proj_main, kr, norm_w, w_ukv, cs)


def _attn_body(tq, q_ref, k_ref, v_ref, nw_ref, o_ref):
    qi = pl.program_id(2)
    q = q_ref[...]

    def block(j, carry, masked):
        m, lsum, acc = carry
        start = pl.multiple_of(j * tq, tq)
        kb = k_ref[pl.ds(start, tq), :]
        vb = v_ref[pl.ds(start, tq), :]
        sc = lax.dot_general(q, kb, (((1,), (1,)), ((), ())), preferred_element_type=F32)
        if masked:
            row = lax.broadcasted_iota(jnp.int32, (tq, tq), 0)
            col = lax.broadcasted_iota(jnp.int32, (tq, tq), 1)
            sc = jnp.where(col <= row, sc, -jnp.inf)
        m_new = jnp.maximum(m, jnp.max(sc, axis=-1, keepdims=True))
        p = jnp.exp(sc - m_new)
        alpha = jnp.exp(m - m_new)
        lsum = alpha * lsum + jnp.sum(p, axis=-1, keepdims=True)
        acc = alpha * acc + jnp.dot(p.astype(BF16), vb, preferred_element_type=F32)
        return m_new, lsum, acc

    init = (jnp.full((tq, 1), -jnp.inf, F32), jnp.zeros((tq, 1), F32), jnp.zeros((tq, V_HEAD), F32))
    carry = lax.fori_loop(0, qi, lambda j, c: block(j, c, False), init)
    _, lsum, acc = block(qi, carry, True)
    o = acc / lsum
    o_ref[...] = _rms_rows(o, nw_ref[...]).astype(BF16)


def _attention(qcat, kcat, v, norm_w, l, b, s):
    nq = s // TQ_ATT
    return pl.pallas_call(
        functools.partial(_attn_body, TQ_ATT),
        grid=(b, MLA_HEADS, nq),
        in_specs=[
            pl.BlockSpec((None, None, TQ_ATT, QK_CAT), lambda bi, hi, i: (bi, hi, i, 0)),
            pl.BlockSpec((None, None, s, QK_CAT), lambda bi, hi, i: (bi, hi, 0, 0)),
            pl.BlockSpec((None, None, s, V_HEAD), lambda bi, hi, i: (bi, hi, 0, 0)),
            pl.BlockSpec((None, 1, V_HEAD), lambda bi, hi, i: (l, 0, 0)),
        ],
        out_specs=pl.BlockSpec((TQ_ATT, V_HEAD), lambda bi, hi, i: (bi * nq + i, hi)),
        out_shape=jax.ShapeDtypeStruct((b * s, MLA_HEADS * V_HEAD), BF16),
        compiler_params=_params(("parallel", "parallel", "arbitrary"), 40),
        name="mla_attention",
    )(qcat, kcat, v, norm_w)


def _conv_body(ts, u_ref, halo_ref, cw_ref, cb_ref, lw_ref, lb_ref, o_ref, hp_ref):
    i = pl.program_id(1)

    def glu(u):
        u = u.astype(F32)
        return u[:, 0:CONV_CH] * _sigmoid(u[:, CONV_CH:2 * CONV_CH])

    hp_ref[CONV_HALO:CONV_HALO + ts, :] = glu(u_ref[...])
    hp_ref[0:CONV_HALO, :] = jnp.where(i == 0, 0.0, glu(halo_ref[...]))
    off = CONV_HALO - (CONV_WIDTH - 1)
    acc = jnp.zeros((ts, CONV_CH), F32)
    for w in range(CONV_WIDTH):
        acc = acc + hp_ref[off + w:off + w + ts, :] * cw_ref[w:w + 1, :]
    acc = acc + cb_ref[...]
    mu = jnp.mean(acc, axis=-1, keepdims=True)
    xc = acc - mu
    var = jnp.mean(xc * xc, axis=-1, keepdims=True)
    y = xc * lax.rsqrt(var + LN_EPS) * lw_ref[...] + lb_ref[...]
    o_ref[...] = (y * _sigmoid(y)).astype(BF16)


def _conv(proj_main, conv_w, conv_b, ln_w, ln_b, l, b, s):
    ns = s // TS_CONV
    per = TS_CONV // CONV_HALO

    def halo_map(bi, i):
        return (jnp.maximum((bi * ns + i) * per - 1, 0), 1)

    return pl.pallas_call(
        functools.partial(_conv_body, TS_CONV),
        grid=(b, ns),
        in_specs=[
            pl.BlockSpec((TS_CONV, 2 * CONV_CH), lambda bi, i: (bi * ns + i, 1)),
            pl.BlockSpec((CONV_HALO, 2 * CONV_CH), halo_map),
            pl.BlockSpec((None, CONV_WIDTH, CONV_CH), lambda bi, i: (l, 0, 0)),
            pl.BlockSpec((None, 1, CONV_CH), lambda bi, i: (l, 0, 0)),
            pl.BlockSpec((None, 1, CONV_CH), lambda bi, i: (l, 0, 0)),
            pl.BlockSpec((None, 1, CONV_CH), lambda bi, i: (l, 0, 0)),
        ],
        out_specs=pl.BlockSpec((TS_CONV, CONV_CH), lambda bi, i: (bi * ns + i, 0)),
        out_shape=jax.ShapeDtypeStruct((b * s, CONV_CH), BF16),
        scratch_shapes=[pltpu.VMEM((CONV_HALO + TS_CONV, CONV_CH), F32)],
        compiler_params=_params(("parallel", "parallel"), 32),
        name="conv_module",
    )(proj_main, proj_main, conv_w, conv_b, ln_w, ln_b)


def _hgrn_tables(tile):
    levels = int(math.log2(tile))
    t = np.arange(tile)
    tri = (t[None, :] <= t[:, None]).astype(np.float32)
    mats = [tri]
    masks = [np.eye(tile, dtype=np.float32)]
    for j in range(levels):
        n = 1 << j
        t0 = (t & ~(2 * n - 1)) + n - 1
        mats.append(tri - (t[None, :] <= t0[:, None]).astype(np.float32))
        same_block = (t[:, None] >> (j + 1)) == (t[None, :] >> (j + 1))
        lower = ((t[:, None] >> j) & 1) == 1
        upper = ((t[None, :] >> j) & 1) == 0
        masks.append((same_block & lower & upper).astype(np.float32))
    return levels, np.concatenate(mats, axis=0), np.stack(masks, axis=0)


def _hgrn_body(tile, levels, hq_ref, hf_ref, hi_ref, hg_ref, lb_ref, nw_ref, mat_ref, mask_ref,
               o_ref, st_ref):
    i = pl.program_id(1)

    @pl.when(i == 0)
    def _():
        st_ref[...] = jnp.zeros_like(st_ref)

    nt = (((1,), (1,)), ((), ()))
    mats = mat_ref[...]
    for hd in range(HGRN_HEADS):
        cols = slice(hd * HGRN_DK, (hd + 1) * HGRN_DK)
        lb = lb_ref[:, cols]
        z = hf_ref[:, cols].astype(F32)
        la = jnp.log(lb)
        lc = jnp.log1p(-lb) + (jnp.minimum(z, 0.0) - jnp.log1p(jnp.exp(-jnp.abs(z))))
        logf = jnp.maximum(la, lc) + jnp.log1p(jnp.exp(-jnp.abs(la - lc)))
        kk = (1.0 - lb) / (1.0 + jnp.exp(z))
        xq = hq_ref[:, cols].astype(F32)
        q = xq * _sigmoid(xq)
        v = hi_ref[:, cols]
        xg = hg_ref[:, cols].astype(F32)
        g = xg * _sigmoid(xg)

        lf_hi = logf.astype(BF16)
        lf_lo = (logf - lf_hi.astype(F32)).astype(BF16)
        dall = (jnp.dot(mats, lf_hi, preferred_element_type=F32)
                + jnp.dot(mats, lf_lo, preferred_element_type=F32))
        bcum = dall[0:tile]
        blast = bcum[tile - 1:tile, :]

        a = mask_ref[0] * lax.dot_general(q.astype(BF16), kk.astype(BF16), nt,
                                          preferred_element_type=F32)
        for j in range(levels):
            wj = jnp.exp(-jnp.abs(dall[(j + 1) * tile:(j + 2) * tile]))
            pj = lax.dot_general((q * wj).astype(BF16), (kk * wj).astype(BF16), nt,
                                 preferred_element_type=F32)
            a = a + mask_ref[j + 1] * pj
        st = st_ref[hd]
        o = jnp.dot(a.astype(BF16), v, preferred_element_type=F32)
        o = o + lax.dot_general((q * jnp.exp(bcum)).astype(BF16), st.astype(BF16), nt,
                                preferred_element_type=F32)
        kdec = (kk * jnp.exp(blast - bcum)).astype(BF16)
        st_ref[hd] = st * jnp.exp(blast) + lax.dot_general(
            v, kdec, (((0,), (0,)), ((), ())), preferred_element_type=F32)
        o_ref[:, cols] = (_rms_rows(o, nw_ref[...]) * g).astype(BF16)


def _hgrn(proj_main, lb, norm_w, l, b, s):
    tile = L_HGRN
    nt = s // tile
    levels, mats, masks = _hgrn_tables(tile)
    width = HGRN_HEADS * HGRN_DK
    row = lambda bi, i: bi * nt + i
    return pl.pallas_call(
        functools.partial(_hgrn_body, tile, levels),
        grid=(b, nt),
        in_specs=[
            pl.BlockSpec((tile, width), lambda bi, i: (row(bi, i), 4)),
            pl.BlockSpec((tile, width), lambda bi, i: (row(bi, i), 5)),
            pl.BlockSpec((tile, width), lambda bi, i: (row(bi, i), 6)),
            pl.BlockSpec((tile, width), lambda bi, i: (row(bi, i), 7)),
            pl.BlockSpec((None, 1, width), lambda bi, i: (l, 0, 0)),
            pl.BlockSpec((None, 1, HGRN_DV), lambda bi, i: (l, 0, 0)),
            pl.BlockSpec(((levels + 1) * tile, tile), lambda bi, i: (0, 0)),
            pl.BlockSpec((levels + 1, tile, tile), lambda bi, i: (0, 0, 0)),
        ],
        out_specs=pl.BlockSpec((tile, width), lambda bi, i: (row(bi, i), 0)),
        out_shape=jax.ShapeDtypeStruct((b * s, width), BF16),
        scratch_shapes=[pltpu.VMEM((HGRN_HEADS, HGRN_DV, HGRN_DK), F32)],
        compiler_params=_params(("parallel", "arbitrary"), 32),
        name="hgrn2",
    )(proj_main, proj_main, proj_main, proj_main, lb, norm_w,
      jnp.asarray(mats, BF16), jnp.asarray(masks, F32))


def _out_proj_body(ya_ref, yb_ref, yc_ref, w_ref, h_ref, o_ref):
    na = ya_ref.shape[1]
    nb = na + yb_ref.shape[1]
    acc = jnp.dot(ya_ref[...], w_ref[0:na, :], preferred_element_type=F32)
    acc = acc + jnp.dot(yb_ref[...], w_ref[na:nb, :], preferred_element_type=F32)
    acc = acc + jnp.dot(yc_ref[...], w_ref[nb:, :], preferred_element_type=F32)
    o_ref[...] = h_ref[...] + acc


def _out_proj(ya, yb, yc, w_out, h, l):
    t, d = h.shape
    dm = w_out.shape[1]
    return pl.pallas_call(
        _out_proj_body,
        grid=(t // TM_OUT, d // TN_OUT),
        in_specs=[
            pl.BlockSpec((TM_OUT, ya.shape[1]), lambda i, j: (i, 0)),
            pl.BlockSpec((TM_OUT, yb.shape[1]), lambda i, j: (i, 0)),
            pl.BlockSpec((TM_OUT, yc.shape[1]), lambda i, j: (i, 0)),
            pl.BlockSpec((None, dm, TN_OUT), lambda i, j: (l, 0, j)),
            pl.BlockSpec((TM_OUT, TN_OUT), lambda i, j: (i, j)),
        ],
        out_specs=pl.BlockSpec((TM_OUT, TN_OUT), lambda i, j: (i, j)),
        out_shape=jax.ShapeDtypeStruct((t, d), F32),
        compiler_params=_params(("parallel", "arbitrary"), 40),
        name="out_proj",
    )(ya, yb, yc, w_out, h)


def _ffn_up_body(x_ref, nw_ref, wg_ref, wu_ref, o_ref, xn_ref):
    @pl.when(pl.program_id(1) == 0)
    def _():
        xn_ref[...] = _rms_rows(x_ref[...], nw_ref[...]).astype(BF16)

    xn = xn_ref[...]
    gate = jnp.dot(xn, wg_ref[...], preferred_element_type=F32)
    up = jnp.dot(xn, wu_ref[...], preferred_element_type=F32)
    o_ref[...] = (gate * _sigmoid(gate) * up).astype(BF16)


def _ffn_up(h, norm_w, w_gate, w_up, l):
    t, d = h.shape
    f = w_gate.shape[2]
    return pl.pallas_call(
        _ffn_up_body,
        grid=(t // TM_UP, f // TF_UP),
        in_specs=[
            pl.BlockSpec((TM_UP, d), lambda i, j: (i, 0)),
            pl.BlockSpec((None, 1, d), lambda i, j: (l, 0, 0)),
            pl.BlockSpec((None, d, TF_UP), lambda i, j: (l, 0, j)),
            pl.BlockSpec((None, d, TF_UP), lambda i, j: (l, 0, j)),
        ],
        out_specs=pl.BlockSpec((TM_UP, TF_UP), lambda i, j: (i, j)),
        out_shape=jax.ShapeDtypeStruct((t, f), BF16),
        scratch_shapes=[pltpu.VMEM((TM_UP, d), BF16)],
        compiler_params=_params(("parallel", "arbitrary"), 40),
        name="ffn_up",
    )(h, norm_w, w_gate, w_up)


def _ffn_down_body(a_ref, w_ref, h_ref, o_ref):
    o_ref[...] = h_ref[...] + jnp.dot(a_ref[...], w_ref[...], preferred_element_type=F32)


def _ffn_down(ff, w_down, h, l):
    t, d = h.shape
    f = ff.shape[1]
    return pl.pallas_call(
        _ffn_down_body,
        grid=(t // TM_DOWN, d // TN_DOWN),
        in_specs=[
            pl.BlockSpec((TM_DOWN, f), lambda i, j: (i, 0)),
            pl.BlockSpec((None, f, TN_DOWN), lambda i, j: (l, 0, j)),
            pl.BlockSpec((TM_DOWN, TN_DOWN), lambda i, j: (i, j)),
        ],
        out_specs=pl.BlockSpec((TM_DOWN, TN_DOWN), lambda i, j: (i, j)),
        out_shape=jax.ShapeDtypeStruct((t, d), F32),
        compiler_params=_params(("parallel", "arbitrary"), 48),
        name="ffn_down",
    )(ff, w_down, h)


def _final_norm_body(x_ref, w_ref, o_ref):
    o_ref[...] = _rms_rows(x_ref[...], w_ref[...])


def _final_norm(h, w):
    t, d = h.shape
    return pl.pallas_call(
        _final_norm_body,
        grid=(t // TM_FINAL,),
        in_specs=[pl.BlockSpec((TM_FINAL, d), lambda i: (i, 0)), pl.BlockSpec((1, d), lambda i: (0, 0))],
        out_specs=pl.BlockSpec((TM_FINAL, d), lambda i: (i, 0)),
        out_shape=jax.ShapeDtypeStruct((t, d), F32),
        compiler_params=_params(("parallel",), 32),
        name="final_norm",
    )(h, w)


def _rope_swap(cols):
    half = QK_ROPE // 2
    return np.concatenate([cols[half:], cols[:half]])


def _w_in_columns():
    kr0 = Q_LORA + KV_LORA
    main = np.concatenate([np.arange(0, kr0), np.arange(kr0 + QK_ROPE, kr0 + QK_ROPE + MAIN_COLS - kr0)])
    kr = np.arange(kr0, kr0 + QK_ROPE)
    return main, np.concatenate([kr, _rope_swap(kr)])


def _w_uq_columns():
    per = QK_NOPE + QK_ROPE
    out = []
    for hd in range(MLA_HEADS):
        pe = np.arange(hd * per + QK_NOPE, (hd + 1) * per)
        out += [np.arange(hd * per, hd * per + QK_NOPE), pe, _rope_swap(pe)]
    return np.concatenate(out)


def _w_ukv_columns():
    per = QK_NOPE + V_HEAD
    kn = [np.arange(hd * per, hd * per + QK_NOPE) for hd in range(MLA_HEADS)]
    vv = [np.arange(hd * per + QK_NOPE, (hd + 1) * per) for hd in range(MLA_HEADS)]
    return np.concatenate(kn + vv)


@jax.jit
def _trunk(x, positions, attn_norm_w, w_in, q_norm_w, w_uq, kv_norm_w, w_ukv, mla_out_norm_w,
           conv_w, conv_b, conv_ln_w, conv_ln_b, hgrn_lower_bounds, hgrn_norm_w, w_out,
           ffn_norm_w, w_gate, w_up, w_down, final_norm_w):
    b, s, d = x.shape
    depth = w_in.shape[0]
    t = b * s

    inv_freq = ROPE_THETA ** (-jnp.arange(0, QK_ROPE, 2, dtype=F32) / QK_ROPE)
    ang = positions.astype(F32)[..., None] * inv_freq
    cos, sin = jnp.cos(ang), jnp.sin(ang)
    cs = jnp.concatenate([cos, cos, -sin, sin], axis=-1).reshape(t, 2 * QK_ROPE)
    lb_all = jnp.cumsum(jax.nn.softmax(hgrn_lower_bounds.astype(F32), axis=0), axis=0)
    lb_all = (lb_all - lb_all[0:1])[:, None, :]

    main_cols, kr_cols = _w_in_columns()
    w_main = w_in[:, :, main_cols].astype(BF16)
    w_kr = w_in[:, :, kr_cols].astype(BF16)
    w_uq_b = w_uq[:, :, _w_uq_columns()].astype(BF16)
    w_ukv_b = w_ukv[:, :, _w_ukv_columns()].astype(BF16)
    w_out_b = w_out.astype(BF16)
    w_gate_b = w_gate.astype(BF16)
    w_up_b = w_up.astype(BF16)
    w_down_b = w_down.astype(BF16)
    row3 = lambda a: a[:, None, :]

    h = x.reshape(t, d)
    for l in range(depth):
        proj_main, kr = _in_proj(h, row3(attn_norm_w), w_main, w_kr, l)
        qcat = _q_proj(proj_main, row3(q_norm_w), w_uq_b, cs, l, b, s)
        kcat, v = _kv_proj(proj_main, kr, row3(kv_norm_w), w_ukv_b, cs, l, b, s)
        ya = _attention(qcat, kcat, v, row3(mla_out_norm_w), l, b, s)
        yb = _conv(proj_main, conv_w, row3(conv_b), row3(conv_ln_w), row3(conv_ln_b), l, b, s)
        yc = _hgrn(proj_main, lb_all, row3(hgrn_norm_w), l, b, s)
        h = _out_proj(ya, yb, yc, w_out_b, h, l)
        ff = _ffn_up(h, row3(ffn_norm_w), w_gate_b, w_up_b, l)
        h = _ffn_down(ff, w_down_b, h, l)
    return _final_norm(h, final_norm_w[None, :]).reshape(b, s, d)


def kernel(x, positions, attn_norm_w, w_in, q_norm_w, w_uq, kv_norm_w, w_ukv, mla_out_norm_w,
           conv_w, conv_b, conv_ln_w, conv_ln_b, hgrn_lower_bounds, hgrn_norm_w, w_out,
           ffn_norm_w, w_gate, w_up, w_down, final_norm_w):
    return _trunk(x, positions, attn_norm_w, w_in, q_norm_w, w_uq, kv_norm_w, w_ukv, mla_out_norm_w,
                  conv_w, conv_b, conv_ln_w, conv_ln_b, hgrn_lower_bounds, hgrn_norm_w, w_out,
                  ffn_norm_w, w_gate, w_up, w_down, final_norm_w)
```

```python
import functools
import math

import jax
import jax.numpy as jnp
import numpy as np
from jax import lax
from jax.experimental import pallas as pl
from jax.experimental.pallas import tpu as pltpu

F32 = jnp.float32
BF16 = jnp.bfloat16

MLA_HEADS = 8
QK_NOPE = 128
QK_ROPE = 64
V_HEAD = 128
Q_LORA = 512
KV_LORA = 512
ROPE_THETA = 10000.0
CONV_CH = 512
CONV_WIDTH = 31
HGRN_HEADS = 4
HGRN_DK = 128
HGRN_DV = 128
RMS_EPS = 1e-6
LN_EPS = 1e-5
QK_CAT = 256
MAIN_COLS = 4096
CONV_HALO = 32

TM_IN = 1024
TN_IN = 512
TS_QKV = 512
TS_CONV = 512
L_HGRN = 128
TM_OUT = 512
TM_UP = 1024
TF_UP = 512
TM_DOWN = 1024
TN_DOWN = 256
TM_FINAL = 512
MIB = 1024 * 1024


def _params(sem, vmem_mib):
    return pltpu.CompilerParams(dimension_semantics=sem, vmem_limit_bytes=vmem_mib * MIB)


def _rms_rows(x, w):
    ms = jnp.mean(x * x, axis=-1, keepdims=True)
    return x * lax.rsqrt(ms + RMS_EPS) * w


def _sigmoid(x):
    return 1.0 / (1.0 + jnp.exp(-x))


def _in_proj_body(nj, x_ref, nw_ref, wm_ref, wk_ref, om_ref, ok_ref, xn_ref):
    j = pl.program_id(1)

    @pl.when(j == 0)
    def _():
        xn_ref[...] = _rms_rows(x_ref[...], nw_ref[...]).astype(BF16)

    @pl.when(j < nj)
    def _():
        om_ref[...] = jnp.dot(xn_ref[...], wm_ref[...], preferred_element_type=F32).astype(BF16)

    @pl.when(j == nj)
    def _():
        ok_ref[...] = jnp.dot(xn_ref[...], wk_ref[...], preferred_element_type=F32)


def _in_proj(h, norm_w, w_main, w_kr, l):
    t, d = h.shape
    nj = MAIN_COLS // TN_IN
    return pl.pallas_call(
        functools.partial(_in_proj_body, nj),
        grid=(t // TM_IN, nj + 1),
        in_specs=[
            pl.BlockSpec((TM_IN, d), lambda i, j: (i, 0)),
            pl.BlockSpec((None, 1, d), lambda i, j: (l, 0, 0)),
            pl.BlockSpec((None, d, TN_IN), lambda i, j: (l, 0, jnp.minimum(j, nj - 1))),
            pl.BlockSpec((None, d, 128), lambda i, j: (l, 0, 0)),
        ],
        out_specs=[
            pl.BlockSpec((TM_IN, TN_IN), lambda i, j: (i, jnp.minimum(j, nj - 1))),
            pl.BlockSpec((TM_IN, 128), lambda i, j: (i, 0)),
        ],
        out_shape=[jax.ShapeDtypeStruct((t, MAIN_COLS), BF16), jax.ShapeDtypeStruct((t, 128), F32)],
        scratch_shapes=[pltpu.VMEM((TM_IN, d), BF16)],
        compiler_params=_params(("parallel", "arbitrary"), 56),
        name="in_proj",
    )(h, norm_w, w_main, w_kr)


_NT = (((1,), (1,)), ((), ()))


def _q_proj_body(scale, c_ref, nw_ref, w_ref, cs_ref, o_ref):
    cn = _rms_rows(c_ref[...].astype(F32), nw_ref[...]).astype(BF16)
    qt = lax.dot_general(w_ref[...], cn, _NT, preferred_element_type=F32)
    cs = cs_ref[...] * scale
    for hd in range(MLA_HEADS):
        base = hd * QK_CAT
        o_ref[hd, 0:QK_NOPE, :] = (qt[base:base + QK_NOPE, :] * scale).astype(BF16)
        o_ref[hd, QK_NOPE:QK_CAT, :] = (qt[base + QK_NOPE:base + QK_CAT, :] * cs).astype(BF16)


def _q_proj(proj_main, norm_w, w_uq_t, cs_t, l, b, s):
    ns = s // TS_QKV
    scale = (QK_NOPE + QK_ROPE) ** -0.5 * math.log2(math.e)
    return pl.pallas_call(
        functools.partial(_q_proj_body, scale),
        grid=(b, ns),
        in_specs=[
            pl.BlockSpec((TS_QKV, Q_LORA), lambda bi, i: (bi * ns + i, 0)),
            pl.BlockSpec((None, 1, Q_LORA), lambda bi, i: (l, 0, 0)),
            pl.BlockSpec((None, MLA_HEADS * QK_CAT, Q_LORA), lambda bi, i: (l, 0, 0)),
            pl.BlockSpec((128, TS_QKV), lambda bi, i: (0, bi * ns + i)),
        ],
        out_specs=pl.BlockSpec((None, MLA_HEADS, QK_CAT, TS_QKV), lambda bi, i: (bi, 0, 0, i)),
        out_shape=jax.ShapeDtypeStruct((b, MLA_HEADS, QK_CAT, s), BF16),
        compiler_params=_params(("parallel", "parallel"), 40),
        name="q_proj",
    )(proj_main, norm_w, w_uq_t, cs_t)


def _kv_proj_body(c_ref, kr_ref, nw_ref, wk_ref, wv_ref, cs_ref, k_ref, v_ref):
    cn = _rms_rows(c_ref[...].astype(F32), nw_ref[...]).astype(BF16)
    kn = jnp.dot(cn, wk_ref[...], preferred_element_type=F32)
    vt = lax.dot_general(wv_ref[...], cn, _NT, preferred_element_type=F32)
    a = kr_ref[...] * cs_ref[...]
    krot = (a + pltpu.roll(a, 64, axis=1)).astype(BF16)
    for hd in range(MLA_HEADS):
        k_ref[hd, :, 0:QK_NOPE] = kn[:, hd * QK_NOPE:(hd + 1) * QK_NOPE].astype(BF16)
        k_ref[hd, :, QK_NOPE:QK_CAT] = krot
        v_ref[hd] = vt[hd * V_HEAD:(hd + 1) * V_HEAD, :].astype(BF16)


def _kv_proj(proj_main, kr, norm_w, w_k, w_v_t, cs, l, b, s):
    ns = s // TS_QKV
    return pl.pallas_call(
        _kv_proj_body,
        grid=(b, ns),
        in_specs=[
            pl.BlockSpec((TS_QKV, KV_LORA), lambda bi, i: (bi * ns + i, 1)),
            pl.BlockSpec((TS_QKV, 128), lambda bi, i: (bi * ns + i, 0)),
            pl.BlockSpec((None, 1, KV_LORA), lambda bi, i: (l, 0, 0)),
            pl.BlockSpec((None, KV_LORA, MLA_HEADS * QK_NOPE), lambda bi, i: (l, 0, 0)),
            pl.BlockSpec((None, MLA_HEADS * V_HEAD, KV_LORA), lambda bi, i: (l, 0, 0)),
            pl.BlockSpec((TS_QKV, 128), lambda bi, i: (bi * ns + i, 0)),
        ],
        out_specs=[
            pl.BlockSpec((None, MLA_HEADS, TS_QKV, QK_CAT), lambda bi, i: (bi, 0, i, 0)),
            pl.BlockSpec((None, MLA_HEADS, None, V_HEAD, TS_QKV), lambda bi, i: (bi, 0, i, 0, 0)),
        ],
        out_shape=[
            jax.ShapeDtypeStruct((b, MLA_HEADS, s, QK_CAT), BF16),
            jax.ShapeDtypeStruct((b, MLA_HEADS, ns, V_HEAD, TS_QKV), BF16),
        ],
        compiler_params=_params(("parallel", "parallel"), 40),
        name="kv_proj",
    )(proj_main, kr, norm_w, w_k, w_v_t, cs)


def _attn_body(tk, q_ref, k_ref, v_ref, nw_ref, o_ref, sa_ref, sb_ref, m_ref, l_ref, acc_ref):
    qi = pl.program_id(2)
    tq = q_ref.shape[1]

    def scores(j):
        start = pl.multiple_of(j * tk, tk)
        return jnp.dot(k_ref[pl.ds(start, tk), :], q_ref[...],
                       preferred_element_type=F32)

    def update(j, sc):
        m = m_ref[...]
        m_new = jnp.maximum(m, jnp.max(sc, axis=0, keepdims=True))
        p = jnp.exp2(sc - m_new)
        alpha = jnp.exp2(m - m_new)
        l_ref[...] = alpha * l_ref[...] + jnp.sum(p, axis=0, keepdims=True)
        acc_ref[...] = alpha * acc_ref[...] + jnp.dot(v_ref[j], p.astype(BF16),
                                                      preferred_element_type=F32)
        m_ref[...] = m_new

    m_ref[...] = jnp.full(m_ref.shape, -jnp.inf, F32)
    l_ref[...] = jnp.zeros(l_ref.shape, F32)
    acc_ref[...] = jnp.zeros(acc_ref.shape, F32)
    sa_ref[...] = scores(0)

    def pair(jj, carry):
        j0 = 2 * jj
        sb_ref[...] = scores(j0 + 1)
        update(j0, sa_ref[...])
        sa_ref[...] = scores(j0 + 2)
        update(j0 + 1, sb_ref[...])
        return carry

    lax.fori_loop(0, qi, pair, 0)
    key = lax.broadcasted_iota(jnp.int32, (tk, tq), 0)
    qry = lax.broadcasted_iota(jnp.int32, (tk, tq), 1)
    sb_ref[...] = scores(2 * qi + 1)
    update(2 * qi, jnp.where(key <= qry, sa_ref[...], -jnp.inf))
    update(2 * qi + 1, jnp.where(key + tk <= qry, sb_ref[...], -jnp.inf))
    o = (acc_ref[...] / l_ref[...]).T
    o_ref[...] = _rms_rows(o, nw_ref[...]).astype(BF16)


def _attention(q_t, kcat, v_t, norm_w, l, b, s):
    tk = TS_QKV
    tq = 2 * tk
    nq = s // tq
    return pl.pallas_call(
        functools.partial(_attn_body, tk),
        grid=(b, MLA_HEADS, nq),
        in_specs=[
            pl.BlockSpec((None, None, QK_CAT, tq), lambda bi, hi, i: (bi, hi, 0, i)),
            pl.BlockSpec((None, None, s, QK_CAT), lambda bi, hi, i: (bi, hi, 0, 0)),
            pl.BlockSpec((None, None, s // tk, V_HEAD, tk), lambda bi, hi, i: (bi, hi, 0, 0, 0)),
            pl.BlockSpec((None, 1, V_HEAD), lambda bi, hi, i: (l, 0, 0)),
        ],
        out_specs=pl.BlockSpec((tq, V_HEAD), lambda bi, hi, i: (bi * nq + i, hi)),
        out_shape=jax.ShapeDtypeStruct((b * s, MLA_HEADS * V_HEAD), BF16),
        scratch_shapes=[
            pltpu.VMEM((tk, tq), F32), pltpu.VMEM((tk, tq), F32),
            pltpu.VMEM((1, tq), F32), pltpu.VMEM((1, tq), F32), pltpu.VMEM((V_HEAD, tq), F32),
        ],
        compiler_params=_params(("parallel", "parallel", "arbitrary"), 40),
        name="mla_attention",
    )(q_t, kcat, v_t, norm_w)


def _conv_body(ts, u_ref, halo_ref, cw_ref, cb_ref, lw_ref, lb_ref, o_ref, hp_ref):
    i = pl.program_id(1)

    def glu(u):
        u = u.astype(F32)
        return u[:, 0:CONV_CH] * _sigmoid(u[:, CONV_CH:2 * CONV_CH])

    hp_ref[CONV_HALO:CONV_HALO + ts, :] = glu(u_ref[...])
    hp_ref[0:CONV_HALO, :] = jnp.where(i == 0, 0.0, glu(halo_ref[...]))
    off = CONV_HALO - (CONV_WIDTH - 1)
    acc = jnp.zeros((ts, CONV_CH), F32)
    hp = hp_ref[...]
    rows = hp.shape[0]
    for r in range(8):
        last = (CONV_WIDTH - 1 - r) // 8
        xr = pltpu.roll(hp, rows - (off + r), axis=0)
        for g in range(last + 1):
            acc = acc + xr[8 * g:8 * g + ts, :] * cw_ref[8 * g + r:8 * g + r + 1, :]
    acc = acc + cb_ref[...]
    mu = jnp.mean(acc, axis=-1, keepdims=True)
    xc = acc - mu
    var = jnp.mean(xc * xc, axis=-1, keepdims=True)
    y = xc * lax.rsqrt(var + LN_EPS) * lw_ref[...] + lb_ref[...]
    o_ref[...] = (y * _sigmoid(y)).astype(BF16)


def _conv(proj_main, conv_w, conv_b, ln_w, ln_b, l, b, s):
    ns = s // TS_CONV
    per = TS_CONV // CONV_HALO

    def halo_map(bi, i):
        return (jnp.maximum((bi * ns + i) * per - 1, 0), 1)

    return pl.pallas_call(
        functools.partial(_conv_body, TS_CONV),
        grid=(b, ns),
        in_specs=[
            pl.BlockSpec((TS_CONV, 2 * CONV_CH), lambda bi, i: (bi * ns + i, 1)),
            pl.BlockSpec((CONV_HALO, 2 * CONV_CH), halo_map),
            pl.BlockSpec((None, CONV_WIDTH, CONV_CH), lambda bi, i: (l, 0, 0)),
            pl.BlockSpec((None, 1, CONV_CH), lambda bi, i: (l, 0, 0)),
            pl.BlockSpec((None, 1, CONV_CH), lambda bi, i: (l, 0, 0)),
            pl.BlockSpec((None, 1, CONV_CH), lambda bi, i: (l, 0, 0)),
        ],
        out_specs=pl.BlockSpec((TS_CONV, CONV_CH), lambda bi, i: (bi * ns + i, 0)),
        out_shape=jax.ShapeDtypeStruct((b * s, CONV_CH), BF16),
        scratch_shapes=[pltpu.VMEM((CONV_HALO + TS_CONV, CONV_CH), F32)],
        compiler_params=_params(("parallel", "parallel"), 32),
        name="conv_module",
    )(proj_main, proj_main, conv_w, conv_b, ln_w, ln_b)


def _hgrn_tables(tile):
    levels = int(math.log2(tile))
    t = np.arange(tile)
    tri = (t[None, :] <= t[:, None]).astype(np.float32)
    mats = [tri]
    masks = [np.eye(tile, dtype=np.float32)]
    for j in range(levels):
        n = 1 << j
        t0 = (t & ~(2 * n - 1)) + n - 1
        mats.append(tri - (t[None, :] <= t0[:, None]).astype(np.float32))
        same_block = (t[:, None] >> (j + 1)) == (t[None, :] >> (j + 1))
        lower = ((t[:, None] >> j) & 1) == 1
        upper = ((t[None, :] >> j) & 1) == 0
        masks.append((same_block & lower & upper).astype(np.float32))
    return levels, np.concatenate(mats, axis=0), np.stack(masks, axis=0)


def _hgrn_body(tile, levels, hq_ref, hf_ref, hi_ref, hg_ref, lb_ref, nw_ref, mat_ref, mask_ref,
               o_ref, st_ref):
    i = pl.program_id(1)

    @pl.when(i == 0)
    def _():
        st_ref[...] = jnp.zeros_like(st_ref)

    nt = (((1,), (1,)), ((), ()))
    mats = mat_ref[...]
    for hd in range(HGRN_HEADS):
        cols = slice(hd * HGRN_DK, (hd + 1) * HGRN_DK)
        lb = lb_ref[:, cols]
        z = hf_ref[:, cols].astype(F32)
        la = jnp.log(lb)
        lc = jnp.log1p(-lb) + (jnp.minimum(z, 0.0) - jnp.log1p(jnp.exp(-jnp.abs(z))))
        logf = jnp.maximum(la, lc) + jnp.log1p(jnp.exp(-jnp.abs(la - lc)))
        kk = (1.0 - lb) / (1.0 + jnp.exp(z))
        xq = hq_ref[:, cols].astype(F32)
        q = xq * _sigmoid(xq)
        v = hi_ref[:, cols]
        xg = hg_ref[:, cols].astype(F32)
        g = xg * _sigmoid(xg)

        lf_hi = logf.astype(BF16)
        lf_lo = (logf - lf_hi.astype(F32)).astype(BF16)
        dall = (jnp.dot(mats, lf_hi, preferred_element_type=F32)
                + jnp.dot(mats, lf_lo, preferred_element_type=F32))
        bcum = dall[0:tile]
        blast = bcum[tile - 1:tile, :]

        a = mask_ref[0] * lax.dot_general(q.astype(BF16), kk.astype(BF16), nt,
                                          preferred_element_type=F32)
        for j in range(levels):
            wj = jnp.exp(-jnp.abs(dall[(j + 1) * tile:(j + 2) * tile]))
            pj = lax.dot_general((q * wj).astype(BF16), (kk * wj).astype(BF16), nt,
                                 preferred_element_type=F32)
            a = a + mask_ref[j + 1] * pj
        st = st_ref[hd]
        o = jnp.dot(a.astype(BF16), v, preferred_element_type=F32)
        o = o + lax.dot_general((q * jnp.exp(bcum)).astype(BF16), st.astype(BF16), nt,
                                preferred_element_type=F32)
        kdec = (kk * jnp.exp(blast - bcum)).astype(BF16)
        st_ref[hd] = st * jnp.exp(blast) + lax.dot_general(
            v, kdec, (((0,), (0,)), ((), ())), preferred_element_type=F32)
        o_ref[:, cols] = (_rms_rows(o, nw_ref[...]) * g).astype(BF16)


def _hgrn(proj_main, lb, norm_w, l, b, s):
    tile = L_HGRN
    nt = s // tile
    levels, mats, masks = _hgrn_tables(tile)
    width = HGRN_HEADS * HGRN_DK
    row = lambda bi, i: bi * nt + i
    return pl.pallas_call(
        functools.partial(_hgrn_body, tile, levels),
        grid=(b, nt),
        in_specs=[
            pl.BlockSpec((tile, width), lambda bi, i: (row(bi, i), 4)),
            pl.BlockSpec((tile, width), lambda bi, i: (row(bi, i), 5)),
            pl.BlockSpec((tile, width), lambda bi, i: (row(bi, i), 6)),
            pl.BlockSpec((tile, width), lambda bi, i: (row(bi, i), 7)),
            pl.BlockSpec((None, 1, width), lambda bi, i: (l, 0, 0)),
            pl.BlockSpec((None, 1, HGRN_DV), lambda bi, i: (l, 0, 0)),
            pl.BlockSpec(((levels + 1) * tile, tile), lambda bi, i: (0, 0)),
            pl.BlockSpec((levels + 1, tile, tile), lambda bi, i: (0, 0, 0)),
        ],
        out_specs=pl.BlockSpec((tile, width), lambda bi, i: (row(bi, i), 0)),
        out_shape=jax.ShapeDtypeStruct((b * s, width), BF16),
        scratch_shapes=[pltpu.VMEM((HGRN_HEADS, HGRN_DV, HGRN_DK), F32)],
        compiler_params=_params(("parallel", "arbitrary"), 32),
        name="hgrn2",
    )(proj_main, proj_main, proj_main, proj_main, lb, norm_w,
      jnp.asarray(mats, BF16), jnp.asarray(masks, F32))


def _out_proj_body(ya_ref, yb_ref, yc_ref, w_ref, h_ref, o_ref):
    na = ya_ref.shape[1]
    nb = na + yb_ref.shape[1]
    acc = jnp.dot(ya_ref[...], w_ref[0:na, :], preferred_element_type=F32)
    acc = acc + jnp.dot(yb_ref[...], w_ref[na:nb, :], preferred_element_type=F32)
    acc = acc + jnp.dot(yc_ref[...], w_ref[nb:, :], preferred_element_type=F32)
    o_ref[...] = h_ref[...] + acc


def _out_proj(ya, yb, yc, w_out, h, l):
    t, d = h.shape
    dm = w_out.shape[1]
    return pl.pallas_call(
        _out_proj_body,
        grid=(t // TM_OUT,),
        in_specs=[
            pl.BlockSpec((TM_OUT, ya.shape[1]), lambda i: (i, 0)),
            pl.BlockSpec((TM_OUT, yb.shape[1]), lambda i: (i, 0)),
            pl.BlockSpec((TM_OUT, yc.shape[1]), lambda i: (i, 0)),
            pl.BlockSpec((None, dm, d), lambda i: (l, 0, 0)),
            pl.BlockSpec((TM_OUT, d), lambda i: (i, 0)),
        ],
        out_specs=pl.BlockSpec((TM_OUT, d), lambda i: (i, 0)),
        out_shape=jax.ShapeDtypeStruct((t, d), F32),
        compiler_params=_params(("parallel",), 56),
        name="out_proj",
    )(ya, yb, yc, w_out, h)


def _ffn_up_body(x_ref, nw_ref, wg_ref, wu_ref, o_ref, xn_ref):
    @pl.when(pl.program_id(1) == 0)
    def _():
        xn_ref[...] = _rms_rows(x_ref[...], nw_ref[...]).astype(BF16)

    xn = xn_ref[...]
    gate = jnp.dot(xn, wg_ref[...], preferred_element_type=F32)
    up = jnp.dot(xn, wu_ref[...], preferred_element_type=F32)
    o_ref[...] = (gate * _sigmoid(gate) * up).astype(BF16)


def _ffn_up(h, norm_w, w_gate, w_up, l):
    t, d = h.shape
    f = w_gate.shape[2]
    return pl.pallas_call(
        _ffn_up_body,
        grid=(t // TM_UP, f // TF_UP),
        in_specs=[
            pl.BlockSpec((TM_UP, d), lambda i, j: (i, 0)),
            pl.BlockSpec((None, 1, d), lambda i, j: (l, 0, 0)),
            pl.BlockSpec((None, d, TF_UP), lambda i, j: (l, 0, j)),
            pl.BlockSpec((None, d, TF_UP), lambda i, j: (l, 0, j)),
        ],
        out_specs=pl.BlockSpec((TM_UP, TF_UP), lambda i, j: (i, j)),
        out_shape=jax.ShapeDtypeStruct((t, f), BF16),
        scratch_shapes=[pltpu.VMEM((TM_UP, d), BF16)],
        compiler_params=_params(("parallel", "arbitrary"), 56),
        name="ffn_up",
    )(h, norm_w, w_gate, w_up)


def _ffn_down_body(a_ref, w_ref, h_ref, o_ref):
    o_ref[...] = h_ref[...] + jnp.dot(a_ref[...], w_ref[...], preferred_element_type=F32)


def _ffn_down(ff, w_down, h, l):
    t, d = h.shape
    f = ff.shape[1]
    return pl.pallas_call(
        _ffn_down_body,
        grid=(t // TM_DOWN, d // TN_DOWN),
        in_specs=[
            pl.BlockSpec((TM_DOWN, f), lambda i, j: (i, 0)),
            pl.BlockSpec((None, f, TN_DOWN), lambda i, j: (l, 0, j)),
            pl.BlockSpec((TM_DOWN, TN_DOWN), lambda i, j: (i, j)),
        ],
        out_specs=pl.BlockSpec((TM_DOWN, TN_DOWN), lambda i, j: (i, j)),
        out_shape=jax.ShapeDtypeStruct((t, d), F32),
        compiler_params=_params(("parallel", "arbitrary"), 48),
        name="ffn_down",
    )(ff, w_down, h)


def _final_norm_body(x_ref, w_ref, o_ref):
    o_ref[...] = _rms_rows(x_ref[...], w_ref[...])


def _final_norm(h, w):
    t, d = h.shape
    return pl.pallas_call(
        _final_norm_body,
        grid=(t // TM_FINAL,),
        in_specs=[pl.BlockSpec((TM_FINAL, d), lambda i: (i, 0)), pl.BlockSpec((1, d), lambda i: (0, 0))],
        out_specs=pl.BlockSpec((TM_FINAL, d), lambda i: (i, 0)),
        out_shape=jax.ShapeDtypeStruct((t, d), F32),
        compiler_params=_params(("parallel",), 32),
        name="final_norm",
    )(h, w)


def _rope_swap(cols):
    half = QK_ROPE // 2
    return np.concatenate([cols[half:], cols[:half]])


def _w_in_columns():
    kr0 = Q_LORA + KV_LORA
    main = np.concatenate([np.arange(0, kr0), np.arange(kr0 + QK_ROPE, kr0 + QK_ROPE + MAIN_COLS - kr0)])
    kr = np.arange(kr0, kr0 + QK_ROPE)
    return main, np.concatenate([kr, _rope_swap(kr)])


def _w_uq_columns():
    per = QK_NOPE + QK_ROPE
    out = []
    for hd in range(MLA_HEADS):
        pe = np.arange(hd * per + QK_NOPE, (hd + 1) * per)
        out += [np.arange(hd * per, hd * per + QK_NOPE), pe, _rope_swap(pe)]
    return np.concatenate(out)


def _w_ukv_columns():
    per = QK_NOPE + V_HEAD
    kn = [np.arange(hd * per, hd * per + QK_NOPE) for hd in range(MLA_HEADS)]
    vv = [np.arange(hd * per + QK_NOPE, (hd + 1) * per) for hd in range(MLA_HEADS)]
    return np.concatenate(kn), np.concatenate(vv)


@jax.jit
def _trunk(x, positions, attn_norm_w, w_in, q_norm_w, w_uq, kv_norm_w, w_ukv, mla_out_norm_w,
           conv_w, conv_b, conv_ln_w, conv_ln_b, hgrn_lower_bounds, hgrn_norm_w, w_out,
           ffn_norm_w, w_gate, w_up, w_down, final_norm_w):
    b, s, d = x.shape
    depth = w_in.shape[0]
    t = b * s

    inv_freq = ROPE_THETA ** (-jnp.arange(0, QK_ROPE, 2, dtype=F32) / QK_ROPE)
    ang = positions.astype(F32)[..., None] * inv_freq
    cos, sin = jnp.cos(ang), jnp.sin(ang)
    cs = jnp.concatenate([cos, cos, -sin, sin], axis=-1).reshape(t, 2 * QK_ROPE)
    cs_t = cs.T
    lb_all = jnp.cumsum(jax.nn.softmax(hgrn_lower_bounds.astype(F32), axis=0), axis=0)
    lb_all = (lb_all - lb_all[0:1])[:, None, :]

    main_cols, kr_cols = _w_in_columns()
    w_main = w_in[:, :, main_cols].astype(BF16)
    w_kr = w_in[:, :, kr_cols].astype(BF16)
    w_uq_t = jnp.swapaxes(w_uq[:, :, _w_uq_columns()], 1, 2).astype(BF16)
    kn_cols, v_cols = _w_ukv_columns()
    w_k_b = w_ukv[:, :, kn_cols].astype(BF16)
    w_v_t = jnp.swapaxes(w_ukv[:, :, v_cols], 1, 2).astype(BF16)
    w_out_b = w_out.astype(BF16)
    w_gate_b = w_gate.astype(BF16)
    w_up_b = w_up.astype(BF16)
    w_down_b = w_down.astype(BF16)
    row3 = lambda a: a[:, None, :]

    h = x.reshape(t, d)
    for l in range(depth):
        proj_main, kr = _in_proj(h, row3(attn_norm_w), w_main, w_kr, l)
        q_t = _q_proj(proj_main, row3(q_norm_w), w_uq_t, cs_t, l, b, s)
        kcat, v_t = _kv_proj(proj_main, kr, row3(kv_norm_w), w_k_b, w_v_t, cs, l, b, s)
        ya = _attention(q_t, kcat, v_t, row3(mla_out_norm_w), l, b, s)
        yb = _conv(proj_main, conv_w, row3(conv_b), row3(conv_ln_w), row3(conv_ln_b), l, b, s)
        yc = _hgrn(proj_main, lb_all, row3(hgrn_norm_w), l, b, s)
        h = _out_proj(ya, yb, yc, w_out_b, h, l)
        ff = _ffn_up(h, row3(ffn_norm_w), w_gate_b, w_up_b, l)
        h = _ffn_down(ff, w_down_b, h, l)
    return _final_norm(h, final_norm_w[None, :]).reshape(b, s, d)


def kernel(x, positions, attn_norm_w, w_in, q_norm_w, w_uq, kv_norm_w, w_ukv, mla_out_norm_w,
           conv_w, conv_b, conv_ln_w, conv_ln_b, hgrn_lower_bounds, hgrn_norm_w, w_out,
           ffn_norm_w, w_gate, w_up, w_down, final_norm_w):
    return _trunk(x, positions, attn_norm_w, w_in, q_norm_w, w_uq, kv_norm_w, w_ukv, mla_out_norm_w,
                  conv_w, conv_b, conv_ln_w, conv_ln_b, hgrn_lower_bounds, hgrn_norm_w, w_out,
                  ffn_norm_w, w_gate, w_up, w_down, final_norm_w)
```

```python
import functools
import math

import jax
import jax.numpy as jnp
import numpy as np
from jax import lax
from jax.experimental import pallas as pl
from jax.experimental.pallas import tpu as pltpu

F32 = jnp.float32
BF16 = jnp.bfloat16

MLA_HEADS = 8
QK_NOPE = 128
QK_ROPE = 64
V_HEAD = 128
Q_LORA = 512
KV_LORA = 512
ROPE_THETA = 10000.0
CONV_CH = 512
CONV_WIDTH = 31
HGRN_HEADS = 4
HGRN_DK = 128
HGRN_DV = 128
RMS_EPS = 1e-6
LN_EPS = 1e-5
QK_CAT = 256
MAIN_COLS = 4096
CONV_HALO = 32

TM_IN = 1024
TR_IN = 512
TN_IN = 512
TS_QKV = 512
TS_CONV = 512
L_HGRN = 128
TM_OUT = 512
TM_UP = 1024
TF_UP = 512
TM_DOWN = 1024
TN_DOWN = 512
TM_FINAL = 512
MIB = 1024 * 1024


def _params(sem, vmem_mib):
    return pltpu.CompilerParams(dimension_semantics=sem, vmem_limit_bytes=vmem_mib * MIB)


def _rms_rows(x, w):
    ms = jnp.mean(x * x, axis=-1, keepdims=True)
    return x * lax.rsqrt(ms + RMS_EPS) * w


def _sigmoid(x):
    return 1.0 / (1.0 + jnp.exp(-x))


def _in_proj_body(x_ref, nw_ref, wm_ref, wk_ref, om_ref, ok_ref):
    for r in range(0, TM_IN, TR_IN):
        xn = _rms_rows(x_ref[r:r + TR_IN, :], nw_ref[...]).astype(BF16)
        for c in range(0, MAIN_COLS, TN_IN):
            om_ref[r:r + TR_IN, c:c + TN_IN] = jnp.dot(
                xn, wm_ref[:, c:c + TN_IN], preferred_element_type=F32).astype(BF16)
        ok_ref[r:r + TR_IN, :] = jnp.dot(xn, wk_ref[...], preferred_element_type=F32)


def _in_proj(h, norm_w, w_main, w_kr, l):
    t, d = h.shape
    once = pl.Buffered(1)
    return pl.pallas_call(
        _in_proj_body,
        grid=(t // TM_IN,),
        in_specs=[
            pl.BlockSpec((TM_IN, d), lambda i: (i, 0)),
            pl.BlockSpec((None, 1, d), lambda i: (l, 0, 0)),
            pl.BlockSpec((None, d, MAIN_COLS), lambda i: (l, 0, 0), pipeline_mode=once),
            pl.BlockSpec((None, d, 128), lambda i: (l, 0, 0), pipeline_mode=once),
        ],
        out_specs=[
            pl.BlockSpec((TM_IN, MAIN_COLS), lambda i: (i, 0)),
            pl.BlockSpec((TM_IN, 128), lambda i: (i, 0)),
        ],
        out_shape=[jax.ShapeDtypeStruct((t, MAIN_COLS), BF16), jax.ShapeDtypeStruct((t, 128), F32)],
        compiler_params=_params(("parallel",), 58),
        name="in_proj",
    )(h, norm_w, w_main, w_kr)


_NT = (((1,), (1,)), ((), ()))


def _q_proj_body(scale, c_ref, nw_ref, w_ref, cs_ref, o_ref):
    cn = _rms_rows(c_ref[...].astype(F32), nw_ref[...]).astype(BF16)
    qt = lax.dot_general(w_ref[...], cn, _NT, preferred_element_type=F32)
    cs = cs_ref[...] * scale
    for hd in range(MLA_HEADS):
        base = hd * QK_CAT
        o_ref[hd, 0:QK_NOPE, :] = (qt[base:base + QK_NOPE, :] * scale).astype(BF16)
        o_ref[hd, QK_NOPE:QK_CAT, :] = (qt[base + QK_NOPE:base + QK_CAT, :] * cs).astype(BF16)


def _q_proj(proj_main, norm_w, w_uq_t, cs_t, l, b, s):
    ns = s // TS_QKV
    scale = (QK_NOPE + QK_ROPE) ** -0.5 * math.log2(math.e)
    return pl.pallas_call(
        functools.partial(_q_proj_body, scale),
        grid=(b, ns),
        in_specs=[
            pl.BlockSpec((TS_QKV, Q_LORA), lambda bi, i: (bi * ns + i, 0)),
            pl.BlockSpec((None, 1, Q_LORA), lambda bi, i: (l, 0, 0)),
            pl.BlockSpec((None, MLA_HEADS * QK_CAT, Q_LORA), lambda bi, i: (l, 0, 0)),
            pl.BlockSpec((128, TS_QKV), lambda bi, i: (0, bi * ns + i)),
        ],
        out_specs=pl.BlockSpec((None, MLA_HEADS, QK_CAT, TS_QKV), lambda bi, i: (bi, 0, 0, i)),
        out_shape=jax.ShapeDtypeStruct((b, MLA_HEADS, QK_CAT, s), BF16),
        compiler_params=_params(("parallel", "parallel"), 40),
        name="q_proj",
    )(proj_main, norm_w, w_uq_t, cs_t)


def _kv_proj_body(c_ref, kr_ref, nw_ref, wk_ref, wv_ref, cs_ref, k_ref, v_ref):
    cn = _rms_rows(c_ref[...].astype(F32), nw_ref[...]).astype(BF16)
    kn = jnp.dot(cn, wk_ref[...], preferred_element_type=F32)
    vt = lax.dot_general(wv_ref[...], cn, _NT, preferred_element_type=F32)
    a = kr_ref[...] * cs_ref[...]
    krot = (a + pltpu.roll(a, 64, axis=1)).astype(BF16)
    for hd in range(MLA_HEADS):
        k_ref[hd, :, 0:QK_NOPE] = kn[:, hd * QK_NOPE:(hd + 1) * QK_NOPE].astype(BF16)
        k_ref[hd, :, QK_NOPE:QK_CAT] = krot
        v_ref[hd] = vt[hd * V_HEAD:(hd + 1) * V_HEAD, :].astype(BF16)


def _kv_proj(proj_main, kr, norm_w, w_k, w_v_t, cs, l, b, s):
    ns = s // TS_QKV
    return pl.pallas_call(
        _kv_proj_body,
        grid=(b, ns),
        in_specs=[
            pl.BlockSpec((TS_QKV, KV_LORA), lambda bi, i: (bi * ns + i, 1)),
            pl.BlockSpec((TS_QKV, 128), lambda bi, i: (bi * ns + i, 0)),
            pl.BlockSpec((None, 1, KV_LORA), lambda bi, i: (l, 0, 0)),
            pl.BlockSpec((None, KV_LORA, MLA_HEADS * QK_NOPE), lambda bi, i: (l, 0, 0)),
            pl.BlockSpec((None, MLA_HEADS * V_HEAD, KV_LORA), lambda bi, i: (l, 0, 0)),
            pl.BlockSpec((TS_QKV, 128), lambda bi, i: (bi * ns + i, 0)),
        ],
        out_specs=[
            pl.BlockSpec((None, MLA_HEADS, TS_QKV, QK_CAT), lambda bi, i: (bi, 0, i, 0)),
            pl.BlockSpec((None, MLA_HEADS, None, V_HEAD, TS_QKV), lambda bi, i: (bi, 0, i, 0, 0)),
        ],
        out_shape=[
            jax.ShapeDtypeStruct((b, MLA_HEADS, s, QK_CAT), BF16),
            jax.ShapeDtypeStruct((b, MLA_HEADS, ns, V_HEAD, TS_QKV), BF16),
        ],
        compiler_params=_params(("parallel", "parallel"), 40),
        name="kv_proj",
    )(proj_main, kr, norm_w, w_k, w_v_t, cs)


def _attn_body(tk, nq, q_ref, k_ref, v_ref, nw_ref, o_ref, sa_ref, sb_ref, m_ref, l_ref, acc_ref):
    tq = 2 * tk
    key = lax.broadcasted_iota(jnp.int32, (tk, tq), 0)
    qry = lax.broadcasted_iota(jnp.int32, (tk, tq), 1)
    causal = key <= qry
    causal_half = causal[:, 0:tk]

    for qi in range(nq):
        q0 = qi * tq

        def scores(j, lo=0, hi=tq):
            start = j * tk if isinstance(j, int) else pl.multiple_of(j * tk, tk)
            return jnp.dot(k_ref[pl.ds(start, tk), :], q_ref[:, q0 + lo:q0 + hi],
                           preferred_element_type=F32)

        def update(j, sc, lo=0, hi=tq):
            m = m_ref[:, lo:hi]
            m_new = jnp.maximum(m, jnp.max(sc, axis=0, keepdims=True))
            p = jnp.exp2(sc - m_new)
            alpha = jnp.exp2(m - m_new)
            l_ref[:, lo:hi] = alpha * l_ref[:, lo:hi] + jnp.sum(p, axis=0, keepdims=True)
            acc_ref[:, lo:hi] = alpha * acc_ref[:, lo:hi] + jnp.dot(
                v_ref[j], p.astype(BF16), preferred_element_type=F32)
            m_ref[:, lo:hi] = m_new

        m_ref[...] = jnp.full(m_ref.shape, -jnp.inf, F32)
        l_ref[...] = jnp.zeros(l_ref.shape, F32)
        acc_ref[...] = jnp.zeros(acc_ref.shape, F32)
        sa_ref[:, 0:tq] = scores(0)

        def pair(jj, carry):
            j0 = 2 * jj
            sb_ref[:, 0:tq] = scores(j0 + 1)
            update(j0, sa_ref[:, 0:tq])
            sa_ref[:, 0:tq] = scores(j0 + 2)
            update(j0 + 1, sb_ref[:, 0:tq])
            return carry

        if qi > 0:
            lax.fori_loop(0, qi, pair, 0)
        sb_ref[:, 0:tk] = scores(2 * qi + 1, tk, tq)
        update(2 * qi, jnp.where(causal, sa_ref[:, 0:tq], -jnp.inf))
        update(2 * qi + 1, jnp.where(causal_half, sb_ref[:, 0:tk], -jnp.inf), tk, tq)
        o = (acc_ref[...] / l_ref[...]).T
        o_ref[q0:q0 + tq, :] = _rms_rows(o, nw_ref[...]).astype(BF16)


def _attention(q_t, kcat, v_t, norm_w, l, b, s):
    tk = TS_QKV
    tq = 2 * tk
    nq = s // tq
    pitch = tq + 128
    return pl.pallas_call(
        functools.partial(_attn_body, tk, nq),
        grid=(b, MLA_HEADS),
        in_specs=[
            pl.BlockSpec((None, None, QK_CAT, s), lambda bi, hi: (bi, hi, 0, 0)),
            pl.BlockSpec((None, None, s, QK_CAT), lambda bi, hi: (bi, hi, 0, 0)),
            pl.BlockSpec((None, None, s // tk, V_HEAD, tk), lambda bi, hi: (bi, hi, 0, 0, 0)),
            pl.BlockSpec((None, 1, V_HEAD), lambda bi, hi: (l, 0, 0)),
        ],
        out_specs=pl.BlockSpec((s, V_HEAD), lambda bi, hi: (bi, hi)),
        out_shape=jax.ShapeDtypeStruct((b * s, MLA_HEADS * V_HEAD), BF16),
        scratch_shapes=[
            pltpu.VMEM((tk, pitch), F32), pltpu.VMEM((tk, pitch), F32),
            pltpu.VMEM((1, tq), F32), pltpu.VMEM((1, tq), F32), pltpu.VMEM((V_HEAD, tq), F32),
        ],
        compiler_params=_params(("parallel", "parallel"), 48),
        name="mla_attention",
    )(q_t, kcat, v_t, norm_w)


def _conv_body(ts, u_ref, halo_ref, cw_ref, cb_ref, lw_ref, lb_ref, o_ref, hp_ref):
    i = pl.program_id(1)

    def glu(u):
        u = u.astype(F32)
        return u[:, 0:CONV_CH] * _sigmoid(u[:, CONV_CH:2 * CONV_CH])

    hp_ref[CONV_HALO:CONV_HALO + ts, :] = glu(u_ref[...])
    hp_ref[0:CONV_HALO, :] = jnp.where(i == 0, 0.0, glu(halo_ref[...]))
    off = CONV_HALO - (CONV_WIDTH - 1)
    acc = jnp.zeros((ts, CONV_CH), F32)
    hp = hp_ref[...]
    rows = hp.shape[0]
    for r in range(8):
        last = (CONV_WIDTH - 1 - r) // 8
        xr = pltpu.roll(hp, rows - (off + r), axis=0)
        for g in range(last + 1):
            acc = acc + xr[8 * g:8 * g + ts, :] * cw_ref[8 * g + r:8 * g + r + 1, :]
    acc = acc + cb_ref[...]
    mu = jnp.mean(acc, axis=-1, keepdims=True)
    xc = acc - mu
    var = jnp.mean(xc * xc, axis=-1, keepdims=True)
    y = xc * lax.rsqrt(var + LN_EPS) * lw_ref[...] + lb_ref[...]
    o_ref[...] = (y * _sigmoid(y)).astype(BF16)


def _conv(proj_main, conv_w, conv_b, ln_w, ln_b, l, b, s):
    ns = s // TS_CONV
    per = TS_CONV // CONV_HALO

    def halo_map(bi, i):
        return (jnp.maximum((bi * ns + i) * per - 1, 0), 1)

    return pl.pallas_call(
        functools.partial(_conv_body, TS_CONV),
        grid=(b, ns),
        in_specs=[
            pl.BlockSpec((TS_CONV, 2 * CONV_CH), lambda bi, i: (bi * ns + i, 1)),
            pl.BlockSpec((CONV_HALO, 2 * CONV_CH), halo_map),
            pl.BlockSpec((None, CONV_WIDTH, CONV_CH), lambda bi, i: (l, 0, 0)),
            pl.BlockSpec((None, 1, CONV_CH), lambda bi, i: (l, 0, 0)),
            pl.BlockSpec((None, 1, CONV_CH), lambda bi, i: (l, 0, 0)),
            pl.BlockSpec((None, 1, CONV_CH), lambda bi, i: (l, 0, 0)),
        ],
        out_specs=pl.BlockSpec((TS_CONV, CONV_CH), lambda bi, i: (bi * ns + i, 0)),
        out_shape=jax.ShapeDtypeStruct((b * s, CONV_CH), BF16),
        scratch_shapes=[pltpu.VMEM((CONV_HALO + TS_CONV, CONV_CH), F32)],
        compiler_params=_params(("parallel", "parallel"), 32),
        name="conv_module",
    )(proj_main, proj_main, conv_w, conv_b, ln_w, ln_b)


def _hgrn_tables(tile):
    levels = int(math.log2(tile))
    t = np.arange(tile)
    tri = (t[None, :] <= t[:, None]).astype(np.float32)
    mats = [tri]
    masks = [np.eye(tile, dtype=np.float32)]
    for j in range(levels):
        n = 1 << j
        t0 = (t & ~(2 * n - 1)) + n - 1
        diff = tri - (t[None, :] <= t0[:, None]).astype(np.float32)
        is_lower = (((t >> j) & 1) == 1)[:, None]
        mats.append(np.where(is_lower, diff, -diff))
        same_block = (t[:, None] >> (j + 1)) == (t[None, :] >> (j + 1))
        lower = ((t[:, None] >> j) & 1) == 1
        upper = ((t[None, :] >> j) & 1) == 0
        masks.append((same_block & lower & upper).astype(np.float32))
    stacked = np.concatenate(mats, axis=0)
    return levels, np.concatenate([stacked, stacked], axis=1), np.stack(masks, axis=0)


def _hgrn_body(tile, levels, hq_ref, hf_ref, hi_ref, hg_ref, lb_ref, nw_ref, mat_ref, mask_ref,
               o_ref, st_ref):
    i = pl.program_id(1)

    @pl.when(i == 0)
    def _():
        st_ref[...] = jnp.zeros_like(st_ref)

    heads = [slice(hd * HGRN_DK, (hd + 1) * HGRN_DK) for hd in range(HGRN_HEADS)]
    lb = lb_ref[...]
    z = hf_ref[...].astype(F32)
    la = jnp.log(lb)
    lc = jnp.log1p(-lb) + (jnp.minimum(z, 0.0) - jnp.log1p(jnp.exp(-jnp.abs(z))))
    logf = jnp.maximum(la, lc) + jnp.log1p(jnp.exp(-jnp.abs(la - lc)))
    kk = (1.0 - lb) / (1.0 + jnp.exp(z))
    xq = hq_ref[...].astype(F32)
    q = xq * _sigmoid(xq)
    xg = hg_ref[...].astype(F32)
    g = xg * _sigmoid(xg)
    v = hi_ref[...]

    lf2 = logf * math.log2(math.e)
    lf_hi = lf2.astype(BF16)
    lf_lo = (lf2 - lf_hi.astype(F32)).astype(BF16)
    dall = jnp.dot(mat_ref[...], jnp.concatenate([lf_hi, lf_lo], axis=0),
                   preferred_element_type=F32)
    bcum = dall[0:tile]
    blast = bcum[tile - 1:tile, :]
    q_b = q.astype(BF16)
    k_b = kk.astype(BF16)

    a = [mask_ref[0] * lax.dot_general(q_b[:, c], k_b[:, c], _NT, preferred_element_type=F32)
         for c in heads]
    for j in range(levels):
        wj = jnp.exp2(dall[(j + 1) * tile:(j + 2) * tile])
        qw = (q * wj).astype(BF16)
        kw = (kk * wj).astype(BF16)
        mj = mask_ref[j + 1]
        a = [a[hd] + mj * lax.dot_general(qw[:, c], kw[:, c], _NT, preferred_element_type=F32)
             for hd, c in enumerate(heads)]
    q_dec = (q * jnp.exp2(bcum)).astype(BF16)
    k_dec = (kk * jnp.exp2(blast - bcum)).astype(BF16)
    s_dec = jnp.exp2(blast)
    for hd, c in enumerate(heads):
        st = st_ref[hd]
        o = jnp.dot(a[hd].astype(BF16), v[:, c], preferred_element_type=F32)
        o = o + lax.dot_general(q_dec[:, c], st.astype(BF16), _NT, preferred_element_type=F32)
        st_ref[hd] = st * s_dec[:, c] + lax.dot_general(
            v[:, c], k_dec[:, c], (((0,), (0,)), ((), ())), preferred_element_type=F32)
        o_ref[:, c] = (_rms_rows(o, nw_ref[...]) * g[:, c]).astype(BF16)


def _hgrn(proj_main, lb, norm_w, l, b, s):
    tile = L_HGRN
    nt = s // tile
    levels, mats, masks = _hgrn_tables(tile)
    width = HGRN_HEADS * HGRN_DK
    row = lambda bi, i: bi * nt + i
    return pl.pallas_call(
        functools.partial(_hgrn_body, tile, levels),
        grid=(b, nt),
        in_specs=[
            pl.BlockSpec((tile, width), lambda bi, i: (row(bi, i), 4)),
            pl.BlockSpec((tile, width), lambda bi, i: (row(bi, i), 5)),
            pl.BlockSpec((tile, width), lambda bi, i: (row(bi, i), 6)),
            pl.BlockSpec((tile, width), lambda bi, i: (row(bi, i), 7)),
            pl.BlockSpec((None, 1, width), lambda bi, i: (l, 0, 0)),
            pl.BlockSpec((None, 1, HGRN_DV), lambda bi, i: (l, 0, 0)),
            pl.BlockSpec(((levels + 1) * tile, 2 * tile), lambda bi, i: (0, 0)),
            pl.BlockSpec((levels + 1, tile, tile), lambda bi, i: (0, 0, 0)),
        ],
        out_specs=pl.BlockSpec((tile, width), lambda bi, i: (row(bi, i), 0)),
        out_shape=jax.ShapeDtypeStruct((b * s, width), BF16),
        scratch_shapes=[pltpu.VMEM((HGRN_HEADS, HGRN_DV, HGRN_DK), F32)],
        compiler_params=_params(("parallel", "arbitrary"), 32),
        name="hgrn2",
    )(proj_main, proj_main, proj_main, proj_main, lb, norm_w,
      jnp.asarray(mats, BF16), jnp.asarray(masks, F32))


def _out_proj_body(ya_ref, yb_ref, yc_ref, w_ref, h_ref, o_ref):
    na = ya_ref.shape[1]
    nb = na + yb_ref.shape[1]
    acc = jnp.dot(ya_ref[...], w_ref[0:na, :], preferred_element_type=F32)
    acc = acc + jnp.dot(yb_ref[...], w_ref[na:nb, :], preferred_element_type=F32)
    acc = acc + jnp.dot(yc_ref[...], w_ref[nb:, :], preferred_element_type=F32)
    o_ref[...] = h_ref[...] + acc


def _out_proj(ya, yb, yc, w_out, h, l):
    t, d = h.shape
    dm = w_out.shape[1]
    return pl.pallas_call(
        _out_proj_body,
        grid=(t // TM_OUT,),
        in_specs=[
            pl.BlockSpec((TM_OUT, ya.shape[1]), lambda i: (i, 0)),
            pl.BlockSpec((TM_OUT, yb.shape[1]), lambda i: (i, 0)),
            pl.BlockSpec((TM_OUT, yc.shape[1]), lambda i: (i, 0)),
            pl.BlockSpec((None, dm, d), lambda i: (l, 0, 0)),
            pl.BlockSpec((TM_OUT, d), lambda i: (i, 0)),
        ],
        out_specs=pl.BlockSpec((TM_OUT, d), lambda i: (i, 0)),
        out_shape=jax.ShapeDtypeStruct((t, d), F32),
        compiler_params=_params(("parallel",), 56),
        name="out_proj",
    )(ya, yb, yc, w_out, h)


def _ffn_up_body(x_ref, nw_ref, wg_ref, wu_ref, o_ref, xn_ref):
    @pl.when(pl.program_id(1) == 0)
    def _():
        xn_ref[...] = _rms_rows(x_ref[...], nw_ref[...]).astype(BF16)

    xn = xn_ref[...]
    gate = jnp.dot(xn, wg_ref[...], preferred_element_type=F32)
    up = jnp.dot(xn, wu_ref[...], preferred_element_type=F32)
    o_ref[...] = (gate * _sigmoid(gate) * up).astype(BF16)


def _ffn_up(h, norm_w, w_gate, w_up, l):
    t, d = h.shape
    nf = w_gate.shape[1]
    f = nf * TF_UP
    return pl.pallas_call(
        _ffn_up_body,
        grid=(t // TM_UP, nf),
        in_specs=[
            pl.BlockSpec((TM_UP, d), lambda i, j: (i, 0)),
            pl.BlockSpec((None, 1, d), lambda i, j: (l, 0, 0)),
            pl.BlockSpec((None, None, d, TF_UP), lambda i, j: (l, j, 0, 0)),
            pl.BlockSpec((None, None, d, TF_UP), lambda i, j: (l, j, 0, 0)),
        ],
        out_specs=pl.BlockSpec((TM_UP, TF_UP), lambda i, j: (i, j)),
        out_shape=jax.ShapeDtypeStruct((t, f), BF16),
        scratch_shapes=[pltpu.VMEM((TM_UP, d), BF16)],
        compiler_params=_params(("parallel", "arbitrary"), 56),
        name="ffn_up",
    )(h, norm_w, w_gate, w_up)


def _ffn_down_body(a_ref, w_ref, h_ref, o_ref):
    o_ref[...] = h_ref[...] + jnp.dot(a_ref[...], w_ref[...], preferred_element_type=F32)


def _ffn_down(ff, w_down, h, l):
    t, d = h.shape
    f = ff.shape[1]
    return pl.pallas_call(
        _ffn_down_body,
        grid=(t // TM_DOWN, d // TN_DOWN),
        in_specs=[
            pl.BlockSpec((TM_DOWN, f), lambda i, j: (i, 0)),
            pl.BlockSpec((None, None, f, TN_DOWN), lambda i, j: (l, j, 0, 0)),
            pl.BlockSpec((TM_DOWN, TN_DOWN), lambda i, j: (i, j)),
        ],
        out_specs=pl.BlockSpec((TM_DOWN, TN_DOWN), lambda i, j: (i, j)),
        out_shape=jax.ShapeDtypeStruct((t, d), F32),
        compiler_params=_params(("parallel", "arbitrary"), 54),
        name="ffn_down",
    )(ff, w_down, h)


def _final_norm_body(x_ref, w_ref, o_ref):
    o_ref[...] = _rms_rows(x_ref[...], w_ref[...])


def _final_norm(h, w):
    t, d = h.shape
    return pl.pallas_call(
        _final_norm_body,
        grid=(t // TM_FINAL,),
        in_specs=[pl.BlockSpec((TM_FINAL, d), lambda i: (i, 0)), pl.BlockSpec((1, d), lambda i: (0, 0))],
        out_specs=pl.BlockSpec((TM_FINAL, d), lambda i: (i, 0)),
        out_shape=jax.ShapeDtypeStruct((t, d), F32),
        compiler_params=_params(("parallel",), 32),
        name="final_norm",
    )(h, w)


def _rope_swap(cols):
    half = QK_ROPE // 2
    return np.concatenate([cols[half:], cols[:half]])


def _w_in_columns():
    kr0 = Q_LORA + KV_LORA
    main = np.concatenate([np.arange(0, kr0), np.arange(kr0 + QK_ROPE, kr0 + QK_ROPE + MAIN_COLS - kr0)])
    kr = np.arange(kr0, kr0 + QK_ROPE)
    return main, np.concatenate([kr, _rope_swap(kr)])


def _w_uq_columns():
    per = QK_NOPE + QK_ROPE
    out = []
    for hd in range(MLA_HEADS):
        pe = np.arange(hd * per + QK_NOPE, (hd + 1) * per)
        out += [np.arange(hd * per, hd * per + QK_NOPE), pe, _rope_swap(pe)]
    return np.concatenate(out)


def _w_ukv_columns():
    per = QK_NOPE + V_HEAD
    kn = [np.arange(hd * per, hd * per + QK_NOPE) for hd in range(MLA_HEADS)]
    vv = [np.arange(hd * per + QK_NOPE, (hd + 1) * per) for hd in range(MLA_HEADS)]
    return np.concatenate(kn), np.concatenate(vv)


@jax.jit
def _trunk(x, positions, attn_norm_w, w_in, q_norm_w, w_uq, kv_norm_w, w_ukv, mla_out_norm_w,
           conv_w, conv_b, conv_ln_w, conv_ln_b, hgrn_lower_bounds, hgrn_norm_w, w_out,
           ffn_norm_w, w_gate, w_up, w_down, final_norm_w):
    b, s, d = x.shape
    depth = w_in.shape[0]
    t = b * s

    inv_freq = ROPE_THETA ** (-jnp.arange(0, QK_ROPE, 2, dtype=F32) / QK_ROPE)
    ang = positions.astype(F32)[..., None] * inv_freq
    cos, sin = jnp.cos(ang), jnp.sin(ang)
    cs = jnp.concatenate([cos, cos, -sin, sin], axis=-1).reshape(t, 2 * QK_ROPE)
    cs_t = cs.T
    lb_all = jnp.cumsum(jax.nn.softmax(hgrn_lower_bounds.astype(F32), axis=0), axis=0)
    lb_all = (lb_all - lb_all[0:1])[:, None, :]

    main_cols, kr_cols = _w_in_columns()
    w_main = w_in[:, :, main_cols].astype(BF16)
    w_kr = w_in[:, :, kr_cols].astype(BF16)
    w_uq_t = jnp.swapaxes(w_uq[:, :, _w_uq_columns()], 1, 2).astype(BF16)
    kn_cols, v_cols = _w_ukv_columns()
    w_k_b = w_ukv[:, :, kn_cols].astype(BF16)
    w_v_t = jnp.swapaxes(w_ukv[:, :, v_cols], 1, 2).astype(BF16)
    w_out_b = w_out.astype(BF16)
    def col_tiles(w, tn):
        dd, k, n = w.shape
        return jnp.swapaxes(w.reshape(dd, k, n // tn, tn), 1, 2).astype(BF16)

    w_gate_b = col_tiles(w_gate, TF_UP)
    w_up_b = col_tiles(w_up, TF_UP)
    w_down_b = col_tiles(w_down, TN_DOWN)
    row3 = lambda a: a[:, None, :]

    h = x.reshape(t, d)
    for l in range(depth):
        proj_main, kr = _in_proj(h, row3(attn_norm_w), w_main, w_kr, l)
        q_t = _q_proj(proj_main, row3(q_norm_w), w_uq_t, cs_t, l, b, s)
        kcat, v_t = _kv_proj(proj_main, kr, row3(kv_norm_w), w_k_b, w_v_t, cs, l, b, s)
        ya = _attention(q_t, kcat, v_t, row3(mla_out_norm_w), l, b, s)
        yb = _conv(proj_main, conv_w, row3(conv_b), row3(conv_ln_w), row3(conv_ln_b), l, b, s)
        yc = _hgrn(proj_main, lb_all, row3(hgrn_norm_w), l, b, s)
        h = _out_proj(ya, yb, yc, w_out_b, h, l)
        ff = _ffn_up(h, row3(ffn_norm_w), w_gate_b, w_up_b, l)
        h = _ffn_down(ff, w_down_b, h, l)
    return _final_norm(h, final_norm_w[None, :]).reshape(b, s, d)


def kernel(x, positions, attn_norm_w, w_in, q_norm_w, w_uq, kv_norm_w, w_ukv, mla_out_norm_w,
           conv_w, conv_b, conv_ln_w, conv_ln_b, hgrn_lower_bounds, hgrn_norm_w, w_out,
           ffn_norm_w, w_gate, w_up, w_down, final_norm_w):
    return _trunk(x, positions, attn_norm_w, w_in, q_norm_w, w_uq, kv_norm_w, w_ukv, mla_out_norm_w,
                  conv_w, conv_b, conv_ln_w, conv_ln_b, hgrn_lower_bounds, hgrn_norm_w, w_out,
                  ffn_norm_w, w_gate, w_up, w_down, final_norm_w)
```

```python
import functools
import math

import jax
import jax.numpy as jnp
import numpy as np
from jax import lax
from jax.experimental import pallas as pl
from jax.experimental.pallas import tpu as pltpu

F32 = jnp.float32
BF16 = jnp.bfloat16

MLA_HEADS = 8
QK_NOPE = 128
QK_ROPE = 64
V_HEAD = 128
Q_LORA = 512
KV_LORA = 512
ROPE_THETA = 10000.0
CONV_CH = 512
CONV_WIDTH = 31
HGRN_HEADS = 4
HGRN_DK = 128
HGRN_DV = 128
RMS_EPS = 1e-6
LN_EPS = 1e-5
QK_CAT = 256
V_ROWS = V_HEAD + 16
MAIN_COLS = 4096
CONV_HALO = 32

TM_IN = 1024
TR_IN = 512
TN_IN = 512
TS_QKV = 512
TS_CONV = 512
L_HGRN = 128
TM_OUT = 512
TM_UP = 1024
TF_UP = 512
TM_DOWN = 1024
TN_DOWN = 512
TM_FINAL = 512
MIB = 1024 * 1024


def _params(sem, vmem_mib):
    return pltpu.CompilerParams(dimension_semantics=sem, vmem_limit_bytes=vmem_mib * MIB)


def _rms_rows(x, w):
    ms = jnp.mean(x * x, axis=-1, keepdims=True)
    return x * lax.rsqrt(ms + RMS_EPS) * w


def _sigmoid(x):
    return 1.0 / (1.0 + jnp.exp(-x))


def _in_proj_body(x_ref, nw_ref, wm_ref, wk_ref, om_ref, ok_ref):
    for r in range(0, TM_IN, TR_IN):
        xn = _rms_rows(x_ref[r:r + TR_IN, :], nw_ref[...]).astype(BF16)
        for c in range(0, MAIN_COLS, TN_IN):
            om_ref[r:r + TR_IN, c:c + TN_IN] = jnp.dot(
                xn, wm_ref[:, c:c + TN_IN], preferred_element_type=F32).astype(BF16)
        ok_ref[r:r + TR_IN, :] = jnp.dot(xn, wk_ref[...], preferred_element_type=F32)


def _in_proj(h, norm_w, w_main, w_kr, l):
    t, d = h.shape
    once = pl.Buffered(1)
    return pl.pallas_call(
        _in_proj_body,
        grid=(t // TM_IN,),
        in_specs=[
            pl.BlockSpec((TM_IN, d), lambda i: (i, 0)),
            pl.BlockSpec((None, 1, d), lambda i: (l, 0, 0)),
            pl.BlockSpec((None, d, MAIN_COLS), lambda i: (l, 0, 0), pipeline_mode=once),
            pl.BlockSpec((None, d, 128), lambda i: (l, 0, 0), pipeline_mode=once),
        ],
        out_specs=[
            pl.BlockSpec((TM_IN, MAIN_COLS), lambda i: (i, 0)),
            pl.BlockSpec((TM_IN, 128), lambda i: (i, 0)),
        ],
        out_shape=[jax.ShapeDtypeStruct((t, MAIN_COLS), BF16), jax.ShapeDtypeStruct((t, 128), F32)],
        compiler_params=_params(("parallel",), 58),
        name="in_proj",
    )(h, norm_w, w_main, w_kr)


_NT = (((1,), (1,)), ((), ()))


def _q_proj_body(scale, c_ref, nw_ref, w_ref, cs_ref, o_ref):
    cn = _rms_rows(c_ref[...].astype(F32), nw_ref[...]).astype(BF16)
    qt = lax.dot_general(w_ref[...], cn, _NT, preferred_element_type=F32)
    cs = cs_ref[...] * scale
    for hd in range(MLA_HEADS):
        base = hd * QK_CAT
        o_ref[hd, 0:QK_NOPE, :] = (qt[base:base + QK_NOPE, :] * scale).astype(BF16)
        o_ref[hd, QK_NOPE:QK_CAT, :] = (qt[base + QK_NOPE:base + QK_CAT, :] * cs).astype(BF16)


def _q_proj(proj_main, norm_w, w_uq_t, cs_t, l, b, s):
    ns = s // TS_QKV
    scale = (QK_NOPE + QK_ROPE) ** -0.5 * math.log2(math.e)
    return pl.pallas_call(
        functools.partial(_q_proj_body, scale),
        grid=(b, ns),
        in_specs=[
            pl.BlockSpec((TS_QKV, Q_LORA), lambda bi, i: (bi * ns + i, 0)),
            pl.BlockSpec((None, 1, Q_LORA), lambda bi, i: (l, 0, 0)),
            pl.BlockSpec((None, MLA_HEADS * QK_CAT, Q_LORA), lambda bi, i: (l, 0, 0)),
            pl.BlockSpec((128, TS_QKV), lambda bi, i: (0, bi * ns + i)),
        ],
        out_specs=pl.BlockSpec((None, MLA_HEADS, QK_CAT, TS_QKV), lambda bi, i: (bi, 0, 0, i)),
        out_shape=jax.ShapeDtypeStruct((b, MLA_HEADS, QK_CAT, s), BF16),
        compiler_params=_params(("parallel", "parallel"), 40),
        name="q_proj",
    )(proj_main, norm_w, w_uq_t, cs_t)


def _kv_proj_body(c_ref, kr_ref, nw_ref, wk_ref, wv_ref, cs_ref, k_ref, v_ref):
    cn = _rms_rows(c_ref[...].astype(F32), nw_ref[...]).astype(BF16)
    kn = jnp.dot(cn, wk_ref[...], preferred_element_type=F32)
    vt = lax.dot_general(wv_ref[...], cn, _NT, preferred_element_type=F32)
    a = kr_ref[...] * cs_ref[...]
    krot = (a + pltpu.roll(a, 64, axis=1)).astype(BF16)
    for hd in range(MLA_HEADS):
        k_ref[hd, :, 0:QK_NOPE] = kn[:, hd * QK_NOPE:(hd + 1) * QK_NOPE].astype(BF16)
        k_ref[hd, :, QK_NOPE:QK_CAT] = krot
        v_ref[hd, 0:V_HEAD, :] = vt[hd * V_HEAD:(hd + 1) * V_HEAD, :].astype(BF16)
        v_ref[hd, V_HEAD:V_ROWS, :] = jnp.ones((V_ROWS - V_HEAD, vt.shape[1]), BF16)


def _kv_proj(proj_main, kr, norm_w, w_k, w_v_t, cs, l, b, s):
    ns = s // TS_QKV
    return pl.pallas_call(
        _kv_proj_body,
        grid=(b, ns),
        in_specs=[
            pl.BlockSpec((TS_QKV, KV_LORA), lambda bi, i: (bi * ns + i, 1)),
            pl.BlockSpec((TS_QKV, 128), lambda bi, i: (bi * ns + i, 0)),
            pl.BlockSpec((None, 1, KV_LORA), lambda bi, i: (l, 0, 0)),
            pl.BlockSpec((None, KV_LORA, MLA_HEADS * QK_NOPE), lambda bi, i: (l, 0, 0)),
            pl.BlockSpec((None, MLA_HEADS * V_HEAD, KV_LORA), lambda bi, i: (l, 0, 0)),
            pl.BlockSpec((TS_QKV, 128), lambda bi, i: (bi * ns + i, 0)),
        ],
        out_specs=[
            pl.BlockSpec((None, MLA_HEADS, TS_QKV, QK_CAT), lambda bi, i: (bi, 0, i, 0)),
            pl.BlockSpec((None, MLA_HEADS, None, V_ROWS, TS_QKV), lambda bi, i: (bi, 0, i, 0, 0)),
        ],
        out_shape=[
            jax.ShapeDtypeStruct((b, MLA_HEADS, s, QK_CAT), BF16),
            jax.ShapeDtypeStruct((b, MLA_HEADS, ns, V_ROWS, TS_QKV), BF16),
        ],
        compiler_params=_params(("parallel", "parallel"), 40),
        name="kv_proj",
    )(proj_main, kr, norm_w, w_k, w_v_t, cs)


def _attn_body(tk, nq, q_ref, k_ref, v_ref, nw_ref, o_ref,
               sa_ref, sb_ref, xa_ref, xb_ref, m_ref, acc_ref):
    tq = 2 * tk
    key = lax.broadcasted_iota(jnp.int32, (tk, tq), 0)
    qry = lax.broadcasted_iota(jnp.int32, (tk, tq), 1)
    causal = key <= qry
    causal_half = causal[:, 0:tk]

    for qi in range(nq):
        q0 = qi * tq

        def fill(s_ref, x_ref, j, lo=0, hi=tq, mask=None):
            start = j * tk if isinstance(j, int) else pl.multiple_of(j * tk, tk)
            sc = jnp.dot(k_ref[pl.ds(start, tk), :], q_ref[:, q0 + lo:q0 + hi],
                         preferred_element_type=F32)
            if mask is not None:
                sc = jnp.where(mask, sc, -jnp.inf)
            s_ref[:, 0:hi - lo] = sc
            x_ref[:, 0:hi - lo] = jnp.max(sc, axis=0, keepdims=True)

        def update(s_ref, x_ref, j, lo=0, hi=tq):
            m = m_ref[:, lo:hi]
            m_new = jnp.maximum(m, x_ref[:, 0:hi - lo])
            p = jnp.exp2(s_ref[:, 0:hi - lo] - m_new)
            alpha = jnp.exp2(m - m_new)
            acc_ref[:, lo:hi] = alpha * acc_ref[:, lo:hi] + jnp.dot(
                v_ref[j], p.astype(BF16), preferred_element_type=F32)
            m_ref[:, lo:hi] = m_new

        m_ref[...] = jnp.full(m_ref.shape, -jnp.inf, F32)
        acc_ref[...] = jnp.zeros(acc_ref.shape, F32)
        first_mask = causal if qi == 0 else None
        fill(sa_ref, xa_ref, 0, mask=first_mask)

        def pair(jj, carry):
            j0 = 2 * jj
            fill(sb_ref, xb_ref, j0 + 1)
            update(sa_ref, xa_ref, j0)
            fill(sa_ref, xa_ref, j0 + 2)
            update(sb_ref, xb_ref, j0 + 1)
            return carry

        if qi > 1:
            lax.fori_loop(0, qi - 1, pair, 0)
        if qi > 0:
            j0 = 2 * (qi - 1)
            fill(sb_ref, xb_ref, j0 + 1)
            update(sa_ref, xa_ref, j0)
            fill(sa_ref, xa_ref, j0 + 2, mask=causal)
            update(sb_ref, xb_ref, j0 + 1)
        fill(sb_ref, xb_ref, 2 * qi + 1, tk, tq, mask=causal_half)
        update(sa_ref, xa_ref, 2 * qi)
        update(sb_ref, xb_ref, 2 * qi + 1, tk, tq)
        o = (acc_ref[0:V_HEAD, :] / acc_ref[V_HEAD:V_HEAD + 1, :]).T
        o_ref[q0:q0 + tq, :] = _rms_rows(o, nw_ref[...]).astype(BF16)


def _attention(q_t, kcat, v_t, norm_w, l, b, s):
    tk = TS_QKV
    tq = 2 * tk
    nq = s // tq
    pitch = tq + 128
    return pl.pallas_call(
        functools.partial(_attn_body, tk, nq),
        grid=(b, MLA_HEADS),
        in_specs=[
            pl.BlockSpec((None, None, QK_CAT, s), lambda bi, hi: (bi, hi, 0, 0)),
            pl.BlockSpec((None, None, s, QK_CAT), lambda bi, hi: (bi, hi, 0, 0)),
            pl.BlockSpec((None, None, s // tk, V_ROWS, tk), lambda bi, hi: (bi, hi, 0, 0, 0)),
            pl.BlockSpec((None, 1, V_HEAD), lambda bi, hi: (l, 0, 0)),
        ],
        out_specs=pl.BlockSpec((s, V_HEAD), lambda bi, hi: (bi, hi)),
        out_shape=jax.ShapeDtypeStruct((b * s, MLA_HEADS * V_HEAD), BF16),
        scratch_shapes=[
            pltpu.VMEM((tk, pitch), F32), pltpu.VMEM((tk, pitch), F32),
            pltpu.VMEM((1, tq), F32), pltpu.VMEM((1, tq), F32),
            pltpu.VMEM((1, tq), F32), pltpu.VMEM((V_ROWS, tq), F32),
        ],
        compiler_params=_params(("parallel", "parallel"), 48),
        name="mla_attention",
    )(q_t, kcat, v_t, norm_w)


def _conv_body(ts, u_ref, halo_ref, cw_ref, cb_ref, lw_ref, lb_ref, o_ref, hp_ref):
    i = pl.program_id(1)

    def glu(u):
        u = u.astype(F32)
        return u[:, 0:CONV_CH] * _sigmoid(u[:, CONV_CH:2 * CONV_CH])

    hp_ref[CONV_HALO:CONV_HALO + ts, :] = glu(u_ref[...])
    hp_ref[0:CONV_HALO, :] = jnp.where(i == 0, 0.0, glu(halo_ref[...]))
    off = CONV_HALO - (CONV_WIDTH - 1)
    acc = jnp.zeros((ts, CONV_CH), F32)
    hp = hp_ref[...]
    rows = hp.shape[0]
    for r in range(8):
        last = (CONV_WIDTH - 1 - r) // 8
        xr = pltpu.roll(hp, rows - (off + r), axis=0)
        for g in range(last + 1):
            acc = acc + xr[8 * g:8 * g + ts, :] * cw_ref[8 * g + r:8 * g + r + 1, :]
    acc = acc + cb_ref[...]
    mu = jnp.mean(acc, axis=-1, keepdims=True)
    xc = acc - mu
    var = jnp.mean(xc * xc, axis=-1, keepdims=True)
    y = xc * lax.rsqrt(var + LN_EPS) * lw_ref[...] + lb_ref[...]
    o_ref[...] = (y * _sigmoid(y)).astype(BF16)


def _conv(proj_main, conv_w, conv_b, ln_w, ln_b, l, b, s):
    ns = s // TS_CONV
    per = TS_CONV // CONV_HALO

    def halo_map(bi, i):
        return (jnp.maximum((bi * ns + i) * per - 1, 0), 1)

    return pl.pallas_call(
        functools.partial(_conv_body, TS_CONV),
        grid=(b, ns),
        in_specs=[
            pl.BlockSpec((TS_CONV, 2 * CONV_CH), lambda bi, i: (bi * ns + i, 1)),
            pl.BlockSpec((CONV_HALO, 2 * CONV_CH), halo_map),
            pl.BlockSpec((None, CONV_WIDTH, CONV_CH), lambda bi, i: (l, 0, 0)),
            pl.BlockSpec((None, 1, CONV_CH), lambda bi, i: (l, 0, 0)),
            pl.BlockSpec((None, 1, CONV_CH), lambda bi, i: (l, 0, 0)),
            pl.BlockSpec((None, 1, CONV_CH), lambda bi, i: (l, 0, 0)),
        ],
        out_specs=pl.BlockSpec((TS_CONV, CONV_CH), lambda bi, i: (bi * ns + i, 0)),
        out_shape=jax.ShapeDtypeStruct((b * s, CONV_CH), BF16),
        scratch_shapes=[pltpu.VMEM((CONV_HALO + TS_CONV, CONV_CH), F32)],
        compiler_params=_params(("parallel", "parallel"), 32),
        name="conv_module",
    )(proj_main, proj_main, conv_w, conv_b, ln_w, ln_b)


def _hgrn_tables(tile):
    levels = int(math.log2(tile))
    t = np.arange(tile)
    tri = (t[None, :] <= t[:, None]).astype(np.float32)
    mats = [tri]
    masks = [np.eye(tile, dtype=np.float32)]
    for j in range(levels):
        n = 1 << j
        t0 = (t & ~(2 * n - 1)) + n - 1
        diff = tri - (t[None, :] <= t0[:, None]).astype(np.float32)
        is_lower = (((t >> j) & 1) == 1)[:, None]
        mats.append(np.where(is_lower, diff, -diff))
        same_block = (t[:, None] >> (j + 1)) == (t[None, :] >> (j + 1))
        lower = ((t[:, None] >> j) & 1) == 1
        upper = ((t[None, :] >> j) & 1) == 0
        masks.append((same_block & lower & upper).astype(np.float32))
    stacked = np.concatenate(mats, axis=0)
    return levels, np.concatenate([stacked, stacked], axis=1), np.stack(masks, axis=0)


def _hgrn_body(tile, levels, hq_ref, hf_ref, hi_ref, hg_ref, lb_ref, nw_ref, mat_ref, mask_ref,
               o_ref, st_ref):
    i = pl.program_id(1)

    @pl.when(i == 0)
    def _():
        st_ref[...] = jnp.zeros_like(st_ref)

    heads = [slice(hd * HGRN_DK, (hd + 1) * HGRN_DK) for hd in range(HGRN_HEADS)]
    lb = lb_ref[...]
    z = hf_ref[...].astype(F32)
    la = jnp.log(lb)
    lc = jnp.log1p(-lb) + (jnp.minimum(z, 0.0) - jnp.log1p(jnp.exp(-jnp.abs(z))))
    logf = jnp.maximum(la, lc) + jnp.log1p(jnp.exp(-jnp.abs(la - lc)))
    kk = (1.0 - lb) / (1.0 + jnp.exp(z))
    xq = hq_ref[...].astype(F32)
    q = xq * _sigmoid(xq)
    xg = hg_ref[...].astype(F32)
    g = xg * _sigmoid(xg)
    v = hi_ref[...]

    lf2 = logf * math.log2(math.e)
    lf_hi = lf2.astype(BF16)
    lf_lo = (lf2 - lf_hi.astype(F32)).astype(BF16)
    dall = jnp.dot(mat_ref[...], jnp.concatenate([lf_hi, lf_lo], axis=0),
                   preferred_element_type=F32)
    bcum = dall[0:tile]
    blast = bcum[tile - 1:tile, :]
    q_b = q.astype(BF16)
    k_b = kk.astype(BF16)

    a = [mask_ref[0] * lax.dot_general(q_b[:, c], k_b[:, c], _NT, preferred_element_type=F32)
         for c in heads]
    for j in range(levels):
        wj = jnp.exp2(dall[(j + 1) * tile:(j + 2) * tile])
        qw = (q * wj).astype(BF16)
        kw = (kk * wj).astype(BF16)
        mj = mask_ref[j + 1]
        a = [a[hd] + mj * lax.dot_general(qw[:, c], kw[:, c], _NT, preferred_element_type=F32)
             for hd, c in enumerate(heads)]
    q_dec = (q * jnp.exp2(bcum)).astype(BF16)
    k_dec = (kk * jnp.exp2(blast - bcum)).astype(BF16)
    s_dec = jnp.exp2(blast)
    for hd, c in enumerate(heads):
        st = st_ref[hd]
        o = jnp.dot(a[hd].astype(BF16), v[:, c], preferred_element_type=F32)
        o = o + lax.dot_general(q_dec[:, c], st.astype(BF16), _NT, preferred_element_type=F32)
        st_ref[hd] = st * s_dec[:, c] + lax.dot_general(
            v[:, c], k_dec[:, c], (((0,), (0,)), ((), ())), preferred_element_type=F32)
        o_ref[:, c] = (_rms_rows(o, nw_ref[...]) * g[:, c]).astype(BF16)


def _hgrn(proj_main, lb, norm_w, l, b, s):
    tile = L_HGRN
    nt = s // tile
    levels, mats, masks = _hgrn_tables(tile)
    width = HGRN_HEADS * HGRN_DK
    row = lambda bi, i: bi * nt + i
    return pl.pallas_call(
        functools.partial(_hgrn_body, tile, levels),
        grid=(b, nt),
        in_specs=[
            pl.BlockSpec((tile, width), lambda bi, i: (row(bi, i), 4)),
            pl.BlockSpec((tile, width), lambda bi, i: (row(bi, i), 5)),
            pl.BlockSpec((tile, width), lambda bi, i: (row(bi, i), 6)),
            pl.BlockSpec((tile, width), lambda bi, i: (row(bi, i), 7)),
            pl.BlockSpec((None, 1, width), lambda bi, i: (l, 0, 0)),
            pl.BlockSpec((None, 1, HGRN_DV), lambda bi, i: (l, 0, 0)),
            pl.BlockSpec(((levels + 1) * tile, 2 * tile), lambda bi, i: (0, 0)),
            pl.BlockSpec((levels + 1, tile, tile), lambda bi, i: (0, 0, 0)),
        ],
        out_specs=pl.BlockSpec((tile, width), lambda bi, i: (row(bi, i), 0)),
        out_shape=jax.ShapeDtypeStruct((b * s, width), BF16),
        scratch_shapes=[pltpu.VMEM((HGRN_HEADS, HGRN_DV, HGRN_DK), F32)],
        compiler_params=_params(("parallel", "arbitrary"), 32),
        name="hgrn2",
    )(proj_main, proj_main, proj_main, proj_main, lb, norm_w,
      jnp.asarray(mats, BF16), jnp.asarray(masks, F32))


def _out_proj_body(ya_ref, yb_ref, yc_ref, w_ref, h_ref, o_ref):
    na = ya_ref.shape[1]
    nb = na + yb_ref.shape[1]
    acc = jnp.dot(ya_ref[...], w_ref[0:na, :], preferred_element_type=F32)
    acc = acc + jnp.dot(yb_ref[...], w_ref[na:nb, :], preferred_element_type=F32)
    acc = acc + jnp.dot(yc_ref[...], w_ref[nb:, :], preferred_element_type=F32)
    o_ref[...] = h_ref[...] + acc


def _out_proj(ya, yb, yc, w_out, h, l):
    t, d = h.shape
    dm = w_out.shape[1]
    return pl.pallas_call(
        _out_proj_body,
        grid=(t // TM_OUT,),
        in_specs=[
            pl.BlockSpec((TM_OUT, ya.shape[1]), lambda i: (i, 0)),
            pl.BlockSpec((TM_OUT, yb.shape[1]), lambda i: (i, 0)),
            pl.BlockSpec((TM_OUT, yc.shape[1]), lambda i: (i, 0)),
            pl.BlockSpec((None, dm, d), lambda i: (l, 0, 0)),
            pl.BlockSpec((TM_OUT, d), lambda i: (i, 0)),
        ],
        out_specs=pl.BlockSpec((TM_OUT, d), lambda i: (i, 0)),
        out_shape=jax.ShapeDtypeStruct((t, d), F32),
        compiler_params=_params(("parallel",), 56),
        name="out_proj",
    )(ya, yb, yc, w_out, h)


def _ffn_up_body(x_ref, nw_ref, wg_ref, wu_ref, o_ref, xn_ref):
    @pl.when(pl.program_id(1) == 0)
    def _():
        xn_ref[...] = _rms_rows(x_ref[...], nw_ref[...]).astype(BF16)

    xn = xn_ref[...]
    gate = jnp.dot(xn, wg_ref[...], preferred_element_type=F32)
    up = jnp.dot(xn, wu_ref[...], preferred_element_type=F32)
    o_ref[...] = (gate * _sigmoid(gate) * up).astype(BF16)


def _ffn_up(h, norm_w, w_gate, w_up, l):
    t, d = h.shape
    f = w_gate.shape[2]
    return pl.pallas_call(
        _ffn_up_body,
        grid=(t // TM_UP, f // TF_UP),
        in_specs=[
            pl.BlockSpec((TM_UP, d), lambda i, j: (i, 0)),
            pl.BlockSpec((None, 1, d), lambda i, j: (l, 0, 0)),
            pl.BlockSpec((None, d, TF_UP), lambda i, j: (l, 0, j)),
            pl.BlockSpec((None, d, TF_UP), lambda i, j: (l, 0, j)),
        ],
        out_specs=pl.BlockSpec((TM_UP, TF_UP), lambda i, j: (i, j)),
        out_shape=jax.ShapeDtypeStruct((t, f), BF16),
        scratch_shapes=[pltpu.VMEM((TM_UP, d), BF16)],
        compiler_params=_params(("parallel", "arbitrary"), 56),
        name="ffn_up",
    )(h, norm_w, w_gate, w_up)


def _ffn_down_body(a_ref, w_ref, h_ref, o_ref):
    o_ref[...] = h_ref[...] + jnp.dot(a_ref[...], w_ref[...], preferred_element_type=F32)


def _ffn_down(ff, w_down, h, l):
    t, d = h.shape
    f = ff.shape[1]
    return pl.pallas_call(
        _ffn_down_body,
        grid=(t // TM_DOWN, d // TN_DOWN),
        in_specs=[
            pl.BlockSpec((TM_DOWN, f), lambda i, j: (i, 0)),
            pl.BlockSpec((None, None, f, TN_DOWN), lambda i, j: (l, j, 0, 0)),
            pl.BlockSpec((TM_DOWN, TN_DOWN), lambda i, j: (i, j)),
        ],
        out_specs=pl.BlockSpec((TM_DOWN, TN_DOWN), lambda i, j: (i, j)),
        out_shape=jax.ShapeDtypeStruct((t, d), F32),
        compiler_params=_params(("parallel", "arbitrary"), 54),
        name="ffn_down",
    )(ff, w_down, h)


def _final_norm_body(x_ref, w_ref, o_ref):
    o_ref[...] = _rms_rows(x_ref[...], w_ref[...])


def _final_norm(h, w):
    t, d = h.shape
    return pl.pallas_call(
        _final_norm_body,
        grid=(t // TM_FINAL,),
        in_specs=[pl.BlockSpec((TM_FINAL, d), lambda i: (i, 0)), pl.BlockSpec((1, d), lambda i: (0, 0))],
        out_specs=pl.BlockSpec((TM_FINAL, d), lambda i: (i, 0)),
        out_shape=jax.ShapeDtypeStruct((t, d), F32),
        compiler_params=_params(("parallel",), 32),
        name="final_norm",
    )(h, w)


def _rope_swap(cols):
    half = QK_ROPE // 2
    return np.concatenate([cols[half:], cols[:half]])


def _w_in_columns():
    kr0 = Q_LORA + KV_LORA
    main = np.concatenate([np.arange(0, kr0), np.arange(kr0 + QK_ROPE, kr0 + QK_ROPE + MAIN_COLS - kr0)])
    kr = np.arange(kr0, kr0 + QK_ROPE)
    return main, np.concatenate([kr, _rope_swap(kr)])


def _w_uq_columns():
    per = QK_NOPE + QK_ROPE
    out = []
    for hd in range(MLA_HEADS):
        pe = np.arange(hd * per + QK_NOPE, (hd + 1) * per)
        out += [np.arange(hd * per, hd * per + QK_NOPE), pe, _rope_swap(pe)]
    return np.concatenate(out)


def _w_ukv_columns():
    per = QK_NOPE + V_HEAD
    kn = [np.arange(hd * per, hd * per + QK_NOPE) for hd in range(MLA_HEADS)]
    vv = [np.arange(hd * per + QK_NOPE, (hd + 1) * per) for hd in range(MLA_HEADS)]
    return np.concatenate(kn), np.concatenate(vv)


@jax.jit
def _trunk(x, positions, attn_norm_w, w_in, q_norm_w, w_uq, kv_norm_w, w_ukv, mla_out_norm_w,
           conv_w, conv_b, conv_ln_w, conv_ln_b, hgrn_lower_bounds, hgrn_norm_w, w_out,
           ffn_norm_w, w_gate, w_up, w_down, final_norm_w):
    b, s, d = x.shape
    depth = w_in.shape[0]
    t = b * s

    inv_freq = ROPE_THETA ** (-jnp.arange(0, QK_ROPE, 2, dtype=F32) / QK_ROPE)
    ang = positions.astype(F32)[..., None] * inv_freq
    cos, sin = jnp.cos(ang), jnp.sin(ang)
    cs = jnp.concatenate([cos, cos, -sin, sin], axis=-1).reshape(t, 2 * QK_ROPE)
    cs_t = cs.T
    lb_all = jnp.cumsum(jax.nn.softmax(hgrn_lower_bounds.astype(F32), axis=0), axis=0)
    lb_all = (lb_all - lb_all[0:1])[:, None, :]

    main_cols, kr_cols = _w_in_columns()
    w_main = w_in[:, :, main_cols].astype(BF16)
    w_kr = w_in[:, :, kr_cols].astype(BF16)
    w_uq_t = jnp.swapaxes(w_uq[:, :, _w_uq_columns()], 1, 2).astype(BF16)
    kn_cols, v_cols = _w_ukv_columns()
    w_k_b = w_ukv[:, :, kn_cols].astype(BF16)
    w_v_t = jnp.swapaxes(w_ukv[:, :, v_cols], 1, 2).astype(BF16)
    w_out_b = w_out.astype(BF16)
    def col_tiles(w, tn):
        dd, k, n = w.shape
        return jnp.swapaxes(w.reshape(dd, k, n // tn, tn), 1, 2).astype(BF16)

    w_gate_b = w_gate.astype(BF16)
    w_up_b = w_up.astype(BF16)
    w_down_b = col_tiles(w_down, TN_DOWN)
    row3 = lambda a: a[:, None, :]

    h = x.reshape(t, d)
    for l in range(depth):
        proj_main, kr = _in_proj(h, row3(attn_norm_w), w_main, w_kr, l)
        q_t = _q_proj(proj_main, row3(q_norm_w), w_uq_t, cs_t, l, b, s)
        kcat, v_t = _kv_proj(proj_main, kr, row3(kv_norm_w), w_k_b, w_v_t, cs, l, b, s)
        ya = _attention(q_t, kcat, v_t, row3(mla_out_norm_w), l, b, s)
        yb = _conv(proj_main, conv_w, row3(conv_b), row3(conv_ln_w), row3(conv_ln_b), l, b, s)
        yc = _hgrn(proj_main, lb_all, row3(hgrn_norm_w), l, b, s)
        h = _out_proj(ya, yb, yc, w_out_b, h, l)
        ff = _ffn_up(h, row3(ffn_norm_w), w_gate_b, w_up_b, l)
        h = _ffn_down(ff, w_down_b, h, l)
    return _final_norm(h, final_norm_w[None, :]).reshape(b, s, d)


def kernel(x, positions, attn_norm_w, w_in, q_norm_w, w_uq, kv_norm_w, w_ukv, mla_out_norm_w,
           conv_w, conv_b, conv_ln_w, conv_ln_b, hgrn_lower_bounds, hgrn_norm_w, w_out,
           ffn_norm_w, w_gate, w_up, w_down, final_norm_w):
    return _trunk(x, positions, attn_norm_w, w_in, q_norm_w, w_uq, kv_norm_w, w_ukv, mla_out_norm_w,
                  conv_w, conv_b, conv_ln_w, conv_ln_b, hgrn_lower_bounds, hgrn_norm_w, w_out,
                  ffn_norm_w, w_gate, w_up, w_down, final_norm_w)
```

```python
import functools
import math

import jax
import jax.numpy as jnp
import numpy as np
from jax import lax
from jax.experimental import pallas as pl
from jax.experimental.pallas import tpu as pltpu

F32 = jnp.float32
BF16 = jnp.bfloat16

MLA_HEADS = 8
QK_NOPE = 128
QK_ROPE = 64
V_HEAD = 128
Q_LORA = 512
KV_LORA = 512
ROPE_THETA = 10000.0
CONV_CH = 512
CONV_WIDTH = 31
HGRN_HEADS = 4
HGRN_DK = 128
HGRN_DV = 128
RMS_EPS = 1e-6
LN_EPS = 1e-5
QK_CAT = 256
V_ROWS = V_HEAD + 16
MAIN_COLS = 4096
CONV_HALO = 32

TM_IN = 1024
TR_IN = 512
TN_IN = 512
TS_QKV = 512
TS_CONV = 512
L_HGRN = 128
TM_OUT = 512
TR_OUT = 256
TM_UP = 2048
TF_UP = 512
TM_DOWN = 1024
TN_DOWN = 512
TM_FINAL = 512
MIB = 1024 * 1024


def _params(sem, vmem_mib):
    return pltpu.CompilerParams(dimension_semantics=sem, vmem_limit_bytes=vmem_mib * MIB)


def _rms_rows(x, w):
    ms = jnp.mean(x * x, axis=-1, keepdims=True)
    return x * lax.rsqrt(ms + RMS_EPS) * w


def _sigmoid(x):
    return 1.0 / (1.0 + jnp.exp(-x))


def _in_proj_body(x_ref, nw_ref, wm_ref, wk_ref, om_ref, ok_ref):
    for r in range(0, TM_IN, TR_IN):
        xn = _rms_rows(x_ref[r:r + TR_IN, :], nw_ref[...]).astype(BF16)
        for c in range(0, MAIN_COLS, TN_IN):
            om_ref[r:r + TR_IN, c:c + TN_IN] = jnp.dot(
                xn, wm_ref[:, c:c + TN_IN], preferred_element_type=F32).astype(BF16)
        ok_ref[r:r + TR_IN, :] = jnp.dot(xn, wk_ref[...], preferred_element_type=F32)


def _in_proj(h, norm_w, w_main, w_kr, l):
    t, d = h.shape
    once = pl.Buffered(1)
    return pl.pallas_call(
        _in_proj_body,
        grid=(t // TM_IN,),
        in_specs=[
            pl.BlockSpec((TM_IN, d), lambda i: (i, 0)),
            pl.BlockSpec((None, 1, d), lambda i: (l, 0, 0)),
            pl.BlockSpec((None, d, MAIN_COLS), lambda i: (l, 0, 0), pipeline_mode=once),
            pl.BlockSpec((None, d, 128), lambda i: (l, 0, 0), pipeline_mode=once),
        ],
        out_specs=[
            pl.BlockSpec((TM_IN, MAIN_COLS), lambda i: (i, 0)),
            pl.BlockSpec((TM_IN, 128), lambda i: (i, 0)),
        ],
        out_shape=[jax.ShapeDtypeStruct((t, MAIN_COLS), BF16), jax.ShapeDtypeStruct((t, 128), F32)],
        compiler_params=_params(("parallel",), 58),
        name="in_proj",
    )(h, norm_w, w_main, w_kr)


_NT = (((1,), (1,)), ((), ()))


def _q_proj_body(scale, c_ref, nw_ref, w_ref, cs_ref, o_ref):
    cn = _rms_rows(c_ref[...].astype(F32), nw_ref[...]).astype(BF16)
    cs = cs_ref[...] * scale
    for hd in range(MLA_HEADS):
        qt = lax.dot_general(w_ref[hd * QK_CAT:(hd + 1) * QK_CAT, :], cn, _NT,
                             preferred_element_type=F32)
        o_ref[hd, 0:QK_NOPE, :] = (qt[0:QK_NOPE, :] * scale).astype(BF16)
        o_ref[hd, QK_NOPE:QK_CAT, :] = (qt[QK_NOPE:QK_CAT, :] * cs).astype(BF16)


def _q_proj(proj_main, norm_w, w_uq_t, cs_t, l, b, s):
    ns = s // TS_QKV
    scale = (QK_NOPE + QK_ROPE) ** -0.5 * math.log2(math.e)
    return pl.pallas_call(
        functools.partial(_q_proj_body, scale),
        grid=(b, ns),
        in_specs=[
            pl.BlockSpec((TS_QKV, Q_LORA), lambda bi, i: (bi * ns + i, 0)),
            pl.BlockSpec((None, 1, Q_LORA), lambda bi, i: (l, 0, 0)),
            pl.BlockSpec((None, MLA_HEADS * QK_CAT, Q_LORA), lambda bi, i: (l, 0, 0)),
            pl.BlockSpec((128, TS_QKV), lambda bi, i: (0, bi * ns + i)),
        ],
        out_specs=pl.BlockSpec((None, MLA_HEADS, QK_CAT, TS_QKV), lambda bi, i: (bi, 0, 0, i)),
        out_shape=jax.ShapeDtypeStruct((b, MLA_HEADS, QK_CAT, s), BF16),
        compiler_params=_params(("parallel", "parallel"), 40),
        name="q_proj",
    )(proj_main, norm_w, w_uq_t, cs_t)


def _kv_proj_body(c_ref, kr_ref, nw_ref, wk_ref, wv_ref, cs_ref, k_ref, v_ref):
    cn = _rms_rows(c_ref[...].astype(F32), nw_ref[...]).astype(BF16)
    a = kr_ref[...] * cs_ref[...]
    krot = (a + pltpu.roll(a, 64, axis=1)).astype(BF16)
    ts = cn.shape[0]
    for h0 in range(0, MLA_HEADS, 2):
        kn = jnp.dot(cn, wk_ref[:, h0 * QK_NOPE:(h0 + 2) * QK_NOPE],
                     preferred_element_type=F32)
        vt = lax.dot_general(wv_ref[h0 * V_HEAD:(h0 + 2) * V_HEAD, :], cn, _NT,
                             preferred_element_type=F32)
        for d in range(2):
            hd = h0 + d
            k_ref[hd, :, 0:QK_NOPE] = kn[:, d * QK_NOPE:(d + 1) * QK_NOPE].astype(BF16)
            k_ref[hd, :, QK_NOPE:QK_CAT] = krot
            v_ref[hd, 0:V_HEAD, :] = vt[d * V_HEAD:(d + 1) * V_HEAD, :].astype(BF16)
            v_ref[hd, V_HEAD:V_ROWS, :] = jnp.ones((V_ROWS - V_HEAD, ts), BF16)


def _kv_proj(proj_main, kr, norm_w, w_k, w_v_t, cs, l, b, s):
    ns = s // TS_QKV
    return pl.pallas_call(
        _kv_proj_body,
        grid=(b, ns),
        in_specs=[
            pl.BlockSpec((TS_QKV, KV_LORA), lambda bi, i: (bi * ns + i, 1)),
            pl.BlockSpec((TS_QKV, 128), lambda bi, i: (bi * ns + i, 0)),
            pl.BlockSpec((None, 1, KV_LORA), lambda bi, i: (l, 0, 0)),
            pl.BlockSpec((None, KV_LORA, MLA_HEADS * QK_NOPE), lambda bi, i: (l, 0, 0)),
            pl.BlockSpec((None, MLA_HEADS * V_HEAD, KV_LORA), lambda bi, i: (l, 0, 0)),
            pl.BlockSpec((TS_QKV, 128), lambda bi, i: (bi * ns + i, 0)),
        ],
        out_specs=[
            pl.BlockSpec((None, MLA_HEADS, TS_QKV, QK_CAT), lambda bi, i: (bi, 0, i, 0)),
            pl.BlockSpec((None, MLA_HEADS, None, V_ROWS, TS_QKV), lambda bi, i: (bi, 0, i, 0, 0)),
        ],
        out_shape=[
            jax.ShapeDtypeStruct((b, MLA_HEADS, s, QK_CAT), BF16),
            jax.ShapeDtypeStruct((b, MLA_HEADS, ns, V_ROWS, TS_QKV), BF16),
        ],
        compiler_params=_params(("parallel", "parallel"), 40),
        name="kv_proj",
    )(proj_main, kr, norm_w, w_k, w_v_t, cs)


def _attn_body(tk, nq, q_ref, k_ref, v_ref, nw_ref, o_ref,
               sa_ref, sb_ref, xa_ref, xb_ref, m_ref, acc_ref):
    tq = 2 * tk
    key = lax.broadcasted_iota(jnp.int32, (tk, tq), 0)
    qry = lax.broadcasted_iota(jnp.int32, (tk, tq), 1)
    causal = key <= qry
    causal_half = causal[:, 0:tk]

    for qi in range(nq):
        q0 = qi * tq

        def fill(s_ref, x_ref, j, lo=0, hi=tq, mask=None):
            start = j * tk if isinstance(j, int) else pl.multiple_of(j * tk, tk)
            sc = jnp.dot(k_ref[pl.ds(start, tk), :], q_ref[:, q0 + lo:q0 + hi],
                         preferred_element_type=F32)
            if mask is not None:
                sc = jnp.where(mask, sc, -jnp.inf)
            s_ref[:, 0:hi - lo] = sc
            x_ref[:, 0:hi - lo] = jnp.max(sc, axis=0, keepdims=True)

        def update(s_ref, x_ref, j, lo=0, hi=tq):
            m = m_ref[:, lo:hi]
            m_new = jnp.maximum(m, x_ref[:, 0:hi - lo])
            p = jnp.exp2(s_ref[:, 0:hi - lo] - m_new)
            alpha = jnp.exp2(m - m_new)
            acc_ref[:, lo:hi] = alpha * acc_ref[:, lo:hi] + jnp.dot(
                v_ref[j], p.astype(BF16), preferred_element_type=F32)
            m_ref[:, lo:hi] = m_new

        m_ref[...] = jnp.full(m_ref.shape, -jnp.inf, F32)
        acc_ref[...] = jnp.zeros(acc_ref.shape, F32)
        first_mask = causal if qi == 0 else None
        fill(sa_ref, xa_ref, 0, mask=first_mask)

        def pair(jj):
            j0 = 2 * jj
            fill(sb_ref, xb_ref, j0 + 1)
            update(sa_ref, xa_ref, j0)
            fill(sa_ref, xa_ref, j0 + 2)
            update(sb_ref, xb_ref, j0 + 1)

        def two_pairs(it, carry):
            pair(2 * it)
            pair(2 * it + 1)
            return carry

        full = max(qi - 1, 0)
        if full >= 2:
            lax.fori_loop(0, full // 2, two_pairs, 0)
        if full % 2:
            pair(full - 1)
        if qi > 0:
            j0 = 2 * (qi - 1)
            fill(sb_ref, xb_ref, j0 + 1)
            update(sa_ref, xa_ref, j0)
            fill(sa_ref, xa_ref, j0 + 2, mask=causal)
            update(sb_ref, xb_ref, j0 + 1)
        fill(sb_ref, xb_ref, 2 * qi + 1, tk, tq, mask=causal_half)
        update(sa_ref, xa_ref, 2 * qi)
        update(sb_ref, xb_ref, 2 * qi + 1, tk, tq)
        o = (acc_ref[0:V_HEAD, :] / acc_ref[V_HEAD:V_HEAD + 1, :]).T
        o_ref[q0:q0 + tq, :] = _rms_rows(o, nw_ref[...]).astype(BF16)


def _attention(q_t, kcat, v_t, norm_w, l, b, s):
    tk = TS_QKV
    tq = 2 * tk
    nq = s // tq
    pitch = tq + 128
    return pl.pallas_call(
        functools.partial(_attn_body, tk, nq),
        grid=(b, MLA_HEADS),
        in_specs=[
            pl.BlockSpec((None, None, QK_CAT, s), lambda bi, hi: (bi, hi, 0, 0)),
            pl.BlockSpec((None, None, s, QK_CAT), lambda bi, hi: (bi, hi, 0, 0)),
            pl.BlockSpec((None, None, s // tk, V_ROWS, tk), lambda bi, hi: (bi, hi, 0, 0, 0)),
            pl.BlockSpec((None, 1, V_HEAD), lambda bi, hi: (l, 0, 0)),
        ],
        out_specs=pl.BlockSpec((s, V_HEAD), lambda bi, hi: (bi, hi)),
        out_shape=jax.ShapeDtypeStruct((b * s, MLA_HEADS * V_HEAD), BF16),
        scratch_shapes=[
            pltpu.VMEM((tk, pitch), F32), pltpu.VMEM((tk, pitch), F32),
            pltpu.VMEM((1, tq), F32), pltpu.VMEM((1, tq), F32),
            pltpu.VMEM((1, tq), F32), pltpu.VMEM((V_ROWS, tq), F32),
        ],
        compiler_params=_params(("parallel", "parallel"), 48),
        name="mla_attention",
    )(q_t, kcat, v_t, norm_w)


def _conv_body(ts, u_ref, halo_ref, cw_ref, cb_ref, lw_ref, lb_ref, o_ref, hp_ref):
    i = pl.program_id(1)

    def glu(u):
        u = u.astype(F32)
        return u[:, 0:CONV_CH] * _sigmoid(u[:, CONV_CH:2 * CONV_CH])

    hp_ref[CONV_HALO:CONV_HALO + ts, :] = glu(u_ref[...])
    hp_ref[0:CONV_HALO, :] = jnp.where(i == 0, 0.0, glu(halo_ref[...]))
    off = CONV_HALO - (CONV_WIDTH - 1)
    acc = jnp.zeros((ts, CONV_CH), F32)
    hp = hp_ref[...]
    rows = hp.shape[0]
    for r in range(8):
        last = (CONV_WIDTH - 1 - r) // 8
        xr = pltpu.roll(hp, rows - (off + r), axis=0)
        for g in range(last + 1):
            acc = acc + xr[8 * g:8 * g + ts, :] * cw_ref[8 * g + r:8 * g + r + 1, :]
    acc = acc + cb_ref[...]
    mu = jnp.mean(acc, axis=-1, keepdims=True)
    xc = acc - mu
    var = jnp.mean(xc * xc, axis=-1, keepdims=True)
    y = xc * lax.rsqrt(var + LN_EPS) * lw_ref[...] + lb_ref[...]
    o_ref[...] = (y * _sigmoid(y)).astype(BF16)


def _conv(proj_main, conv_w, conv_b, ln_w, ln_b, l, b, s):
    ns = s // TS_CONV
    per = TS_CONV // CONV_HALO

    def halo_map(bi, i):
        return (jnp.maximum((bi * ns + i) * per - 1, 0), 1)

    return pl.pallas_call(
        functools.partial(_conv_body, TS_CONV),
        grid=(b, ns),
        in_specs=[
            pl.BlockSpec((TS_CONV, 2 * CONV_CH), lambda bi, i: (bi * ns + i, 1)),
            pl.BlockSpec((CONV_HALO, 2 * CONV_CH), halo_map),
            pl.BlockSpec((None, CONV_WIDTH, CONV_CH), lambda bi, i: (l, 0, 0)),
            pl.BlockSpec((None, 1, CONV_CH), lambda bi, i: (l, 0, 0)),
            pl.BlockSpec((None, 1, CONV_CH), lambda bi, i: (l, 0, 0)),
            pl.BlockSpec((None, 1, CONV_CH), lambda bi, i: (l, 0, 0)),
        ],
        out_specs=pl.BlockSpec((TS_CONV, CONV_CH), lambda bi, i: (bi * ns + i, 0)),
        out_shape=jax.ShapeDtypeStruct((b * s, CONV_CH), BF16),
        scratch_shapes=[pltpu.VMEM((CONV_HALO + TS_CONV, CONV_CH), F32)],
        compiler_params=_params(("parallel", "parallel"), 32),
        name="conv_module",
    )(proj_main, proj_main, conv_w, conv_b, ln_w, ln_b)


def _hgrn_tables(tile):
    levels = int(math.log2(tile))
    t = np.arange(tile)
    tri = (t[None, :] <= t[:, None]).astype(np.float32)
    mats = [tri]
    masks = [np.eye(tile, dtype=np.float32)]
    for j in range(levels):
        n = 1 << j
        t0 = (t & ~(2 * n - 1)) + n - 1
        diff = tri - (t[None, :] <= t0[:, None]).astype(np.float32)
        is_lower = (((t >> j) & 1) == 1)[:, None]
        mats.append(np.where(is_lower, diff, -diff))
        same_block = (t[:, None] >> (j + 1)) == (t[None, :] >> (j + 1))
        lower = ((t[:, None] >> j) & 1) == 1
        upper = ((t[None, :] >> j) & 1) == 0
        masks.append((same_block & lower & upper).astype(np.float32))
    stacked = np.concatenate(mats, axis=0)
    return levels, np.concatenate([stacked, stacked], axis=1), np.stack(masks, axis=0)


def _hgrn_body(tile, levels, hq_ref, hf_ref, hi_ref, hg_ref, lb_ref, nw_ref, mat_ref, mask_ref,
               o_ref, st_ref):
    i = pl.program_id(1)

    @pl.when(i == 0)
    def _():
        st_ref[...] = jnp.zeros_like(st_ref)

    heads = [slice(hd * HGRN_DK, (hd + 1) * HGRN_DK) for hd in range(HGRN_HEADS)]
    lb = lb_ref[...]
    z = hf_ref[...].astype(F32)
    la = jnp.log(lb)
    lc = jnp.log1p(-lb) + (jnp.minimum(z, 0.0) - jnp.log1p(jnp.exp(-jnp.abs(z))))
    logf = jnp.maximum(la, lc) + jnp.log1p(jnp.exp(-jnp.abs(la - lc)))
    kk = (1.0 - lb) / (1.0 + jnp.exp(z))
    xq = hq_ref[...].astype(F32)
    q = xq * _sigmoid(xq)
    xg = hg_ref[...].astype(F32)
    g = xg * _sigmoid(xg)
    v = hi_ref[...]

    lf2 = logf * math.log2(math.e)
    lf_hi = lf2.astype(BF16)
    lf_lo = (lf2 - lf_hi.astype(F32)).astype(BF16)
    dall = jnp.dot(mat_ref[...], jnp.concatenate([lf_hi, lf_lo], axis=0),
                   preferred_element_type=F32)
    bcum = dall[0:tile]
    blast = bcum[tile - 1:tile, :]
    q_b = q.astype(BF16)
    k_b = kk.astype(BF16)

    a = [mask_ref[0] * lax.dot_general(q_b[:, c], k_b[:, c], _NT, preferred_element_type=F32)
         for c in heads]
    for j in range(levels):
        wj = jnp.exp2(dall[(j + 1) * tile:(j + 2) * tile])
        qw = (q * wj).astype(BF16)
        kw = (kk * wj).astype(BF16)
        mj = mask_ref[j + 1]
        a = [a[hd] + mj * lax.dot_general(qw[:, c], kw[:, c], _NT, preferred_element_type=F32)
             for hd, c in enumerate(heads)]
    q_dec = (q * jnp.exp2(bcum)).astype(BF16)
    k_dec = (kk * jnp.exp2(blast - bcum)).astype(BF16)
    s_dec = jnp.exp2(blast)
    for hd, c in enumerate(heads):
        st = st_ref[hd]
        o = jnp.dot(a[hd].astype(BF16), v[:, c], preferred_element_type=F32)
        o = o + lax.dot_general(q_dec[:, c], st.astype(BF16), _NT, preferred_element_type=F32)
        st_ref[hd] = st * s_dec[:, c] + lax.dot_general(
            v[:, c], k_dec[:, c], (((0,), (0,)), ((), ())), preferred_element_type=F32)
        o_ref[:, c] = (_rms_rows(o, nw_ref[...]) * g[:, c]).astype(BF16)


def _hgrn(proj_main, lb, norm_w, l, b, s):
    tile = L_HGRN
    nt = s // tile
    levels, mats, masks = _hgrn_tables(tile)
    width = HGRN_HEADS * HGRN_DK
    row = lambda bi, i: bi * nt + i
    return pl.pallas_call(
        functools.partial(_hgrn_body, tile, levels),
        grid=(b, nt),
        in_specs=[
            pl.BlockSpec((tile, width), lambda bi, i: (row(bi, i), 4)),
            pl.BlockSpec((tile, width), lambda bi, i: (row(bi, i), 5)),
            pl.BlockSpec((tile, width), lambda bi, i: (row(bi, i), 6)),
            pl.BlockSpec((tile, width), lambda bi, i: (row(bi, i), 7)),
            pl.BlockSpec((None, 1, width), lambda bi, i: (l, 0, 0)),
            pl.BlockSpec((None, 1, HGRN_DV), lambda bi, i: (l, 0, 0)),
            pl.BlockSpec(((levels + 1) * tile, 2 * tile), lambda bi, i: (0, 0)),
            pl.BlockSpec((levels + 1, tile, tile), lambda bi, i: (0, 0, 0)),
        ],
        out_specs=pl.BlockSpec((tile, width), lambda bi, i: (row(bi, i), 0)),
        out_shape=jax.ShapeDtypeStruct((b * s, width), BF16),
        scratch_shapes=[pltpu.VMEM((HGRN_HEADS, HGRN_DV, HGRN_DK), F32)],
        compiler_params=_params(("parallel", "arbitrary"), 32),
        name="hgrn2",
    )(proj_main, proj_main, proj_main, proj_main, lb, norm_w,
      jnp.asarray(mats, BF16), jnp.asarray(masks, F32))


def _out_proj_body(ya_ref, yb_ref, yc_ref, w_ref, h_ref, nw_ref, o_ref, xn_ref):
    na = ya_ref.shape[1]
    nb = na + yb_ref.shape[1]
    for r in range(0, TM_OUT, TR_OUT):
        rows = slice(r, r + TR_OUT)
        acc = jnp.dot(ya_ref[rows, :], w_ref[0:na, :], preferred_element_type=F32)
        acc = acc + jnp.dot(yb_ref[rows, :], w_ref[na:nb, :], preferred_element_type=F32)
        acc = acc + jnp.dot(yc_ref[rows, :], w_ref[nb:, :], preferred_element_type=F32)
        hn = h_ref[rows, :] + acc
        o_ref[rows, :] = hn
        xn_ref[rows, :] = _rms_rows(hn, nw_ref[...]).astype(BF16)


def _out_proj(ya, yb, yc, w_out, h, ffn_norm_w, l):
    t, d = h.shape
    dm = w_out.shape[1]
    return pl.pallas_call(
        _out_proj_body,
        grid=(t // TM_OUT,),
        in_specs=[
            pl.BlockSpec((TM_OUT, ya.shape[1]), lambda i: (i, 0)),
            pl.BlockSpec((TM_OUT, yb.shape[1]), lambda i: (i, 0)),
            pl.BlockSpec((TM_OUT, yc.shape[1]), lambda i: (i, 0)),
            pl.BlockSpec((None, dm, d), lambda i: (l, 0, 0), pipeline_mode=pl.Buffered(1)),
            pl.BlockSpec((TM_OUT, d), lambda i: (i, 0)),
            pl.BlockSpec((None, 1, d), lambda i: (l, 0, 0)),
        ],
        out_specs=[pl.BlockSpec((TM_OUT, d), lambda i: (i, 0)), pl.BlockSpec((TM_OUT, d), lambda i: (i, 0))],
        out_shape=[jax.ShapeDtypeStruct((t, d), F32), jax.ShapeDtypeStruct((t, d), BF16)],
        compiler_params=_params(("parallel",), 56),
        name="out_proj",
    )(ya, yb, yc, w_out, h, ffn_norm_w)


def _ffn_up_body(xn_ref, wg_ref, wu_ref, o_ref):
    xn = xn_ref[...]
    for c in range(0, TF_UP, 256):
        gate = jnp.dot(xn, wg_ref[:, c:c + 256], preferred_element_type=F32)
        up = jnp.dot(xn, wu_ref[:, c:c + 256], preferred_element_type=F32)
        o_ref[:, c:c + 256] = (gate * _sigmoid(gate) * up).astype(BF16)


def _ffn_up(xn, w_gate, w_up, l):
    t, d = xn.shape
    f = w_gate.shape[2]
    return pl.pallas_call(
        _ffn_up_body,
        grid=(t // TM_UP, f // TF_UP),
        in_specs=[
            pl.BlockSpec((TM_UP, d), lambda i, j: (i, 0)),
            pl.BlockSpec((None, d, TF_UP), lambda i, j: (l, 0, j)),
            pl.BlockSpec((None, d, TF_UP), lambda i, j: (l, 0, j)),
        ],
        out_specs=pl.BlockSpec((TM_UP, TF_UP), lambda i, j: (i, j)),
        out_shape=jax.ShapeDtypeStruct((t, f), BF16),
        compiler_params=_params(("parallel", "arbitrary"), 40),
        name="ffn_up",
    )(xn, w_gate, w_up)


def _ffn_down_body(a_ref, w_ref, h_ref, o_ref):
    o_ref[...] = h_ref[...] + jnp.dot(a_ref[...], w_ref[...], preferred_element_type=F32)


def _ffn_down(ff, w_down, h, l):
    t, d = h.shape
    f = ff.shape[1]
    return pl.pallas_call(
        _ffn_down_body,
        grid=(t // TM_DOWN, d // TN_DOWN),
        in_specs=[
            pl.BlockSpec((TM_DOWN, f), lambda i, j: (i, 0)),
            pl.BlockSpec((None, None, f, TN_DOWN), lambda i, j: (l, j, 0, 0)),
            pl.BlockSpec((TM_DOWN, TN_DOWN), lambda i, j: (i, j)),
        ],
        out_specs=pl.BlockSpec((TM_DOWN, TN_DOWN), lambda i, j: (i, j)),
        out_shape=jax.ShapeDtypeStruct((t, d), F32),
        compiler_params=_params(("parallel", "arbitrary"), 54),
        name="ffn_down",
    )(ff, w_down, h)


def _final_norm_body(x_ref, w_ref, o_ref):
    o_ref[...] = _rms_rows(x_ref[...], w_ref[...])


def _final_norm(h, w):
    t, d = h.shape
    return pl.pallas_call(
        _final_norm_body,
        grid=(t // TM_FINAL,),
        in_specs=[pl.BlockSpec((TM_FINAL, d), lambda i: (i, 0)), pl.BlockSpec((1, d), lambda i: (0, 0))],
        out_specs=pl.BlockSpec((TM_FINAL, d), lambda i: (i, 0)),
        out_shape=jax.ShapeDtypeStruct((t, d), F32),
        compiler_params=_params(("parallel",), 32),
        name="final_norm",
    )(h, w)


def _rope_swap(cols):
    half = QK_ROPE // 2
    return np.concatenate([cols[half:], cols[:half]])


def _w_in_columns():
    kr0 = Q_LORA + KV_LORA
    main = np.concatenate([np.arange(0, kr0), np.arange(kr0 + QK_ROPE, kr0 + QK_ROPE + MAIN_COLS - kr0)])
    kr = np.arange(kr0, kr0 + QK_ROPE)
    return main, np.concatenate([kr, _rope_swap(kr)])


def _w_uq_columns():
    per = QK_NOPE + QK_ROPE
    out = []
    for hd in range(MLA_HEADS):
        pe = np.arange(hd * per + QK_NOPE, (hd + 1) * per)
        out += [np.arange(hd * per, hd * per + QK_NOPE), pe, _rope_swap(pe)]
    return np.concatenate(out)


def _w_ukv_columns():
    per = QK_NOPE + V_HEAD
    kn = [np.arange(hd * per, hd * per + QK_NOPE) for hd in range(MLA_HEADS)]
    vv = [np.arange(hd * per + QK_NOPE, (hd + 1) * per) for hd in range(MLA_HEADS)]
    return np.concatenate(kn), np.concatenate(vv)


@jax.jit
def _trunk(x, positions, attn_norm_w, w_in, q_norm_w, w_uq, kv_norm_w, w_ukv, mla_out_norm_w,
           conv_w, conv_b, conv_ln_w, conv_ln_b, hgrn_lower_bounds, hgrn_norm_w, w_out,
           ffn_norm_w, w_gate, w_up, w_down, final_norm_w):
    b, s, d = x.shape
    depth = w_in.shape[0]
    t = b * s

    inv_freq = ROPE_THETA ** (-jnp.arange(0, QK_ROPE, 2, dtype=F32) / QK_ROPE)
    ang = positions.astype(F32)[..., None] * inv_freq
    cos, sin = jnp.cos(ang), jnp.sin(ang)
    cs = jnp.concatenate([cos, cos, -sin, sin], axis=-1).reshape(t, 2 * QK_ROPE)
    cs_t = cs.T
    lb_all = jnp.cumsum(jax.nn.softmax(hgrn_lower_bounds.astype(F32), axis=0), axis=0)
    lb_all = (lb_all - lb_all[0:1])[:, None, :]

    main_cols, kr_cols = _w_in_columns()
    w_main = w_in[:, :, main_cols].astype(BF16)
    w_kr = w_in[:, :, kr_cols].astype(BF16)
    w_uq_t = jnp.swapaxes(w_uq[:, :, _w_uq_columns()], 1, 2).astype(BF16)
    kn_cols, v_cols = _w_ukv_columns()
    w_k_b = w_ukv[:, :, kn_cols].astype(BF16)
    w_v_t = jnp.swapaxes(w_ukv[:, :, v_cols], 1, 2).astype(BF16)
    w_out_b = w_out.astype(BF16)
    def col_tiles(w, tn):
        dd, k, n = w.shape
        return jnp.swapaxes(w.reshape(dd, k, n // tn, tn), 1, 2).astype(BF16)

    w_gate_b = w_gate.astype(BF16)
    w_up_b = w_up.astype(BF16)
    w_down_b = col_tiles(w_down, TN_DOWN)
    row3 = lambda a: a[:, None, :]

    h = x.reshape(t, d)
    for l in range(depth):
        proj_main, kr = _in_proj(h, row3(attn_norm_w), w_main, w_kr, l)
        q_t = _q_proj(proj_main, row3(q_norm_w), w_uq_t, cs_t, l, b, s)
        kcat, v_t = _kv_proj(proj_main, kr, row3(kv_norm_w), w_k_b, w_v_t, cs, l, b, s)
        ya = _attention(q_t, kcat, v_t, row3(mla_out_norm_w), l, b, s)
        yb = _conv(proj_main, conv_w, row3(conv_b), row3(conv_ln_w), row3(conv_ln_b), l, b, s)
        yc = _hgrn(proj_main, lb_all, row3(hgrn_norm_w), l, b, s)
        h, xn = _out_proj(ya, yb, yc, w_out_b, h, row3(ffn_norm_w), l)
        ff = _ffn_up(xn, w_gate_b, w_up_b, l)
        h = _ffn_down(ff, w_down_b, h, l)
    return _final_norm(h, final_norm_w[None, :]).reshape(b, s, d)


def kernel(x, positions, attn_norm_w, w_in, q_norm_w, w_uq, kv_norm_w, w_ukv, mla_out_norm_w,
           conv_w, conv_b, conv_ln_w, conv_ln_b, hgrn_lower_bounds, hgrn_norm_w, w_out,
           ffn_norm_w, w_gate, w_up, w_down, final_norm_w):
    return _trunk(x, positions, attn_norm_w, w_in, q_norm_w, w_uq, kv_norm_w, w_ukv, mla_out_norm_w,
                  conv_w, conv_b, conv_ln_w, conv_ln_b, hgrn_lower_bounds, hgrn_norm_w, w_out,
                  ffn_norm_w, w_gate, w_up, w_down, final_norm_w)
```

```python
import functools
import math

import jax
import jax.numpy as jnp
import numpy as np
from jax import lax
from jax.experimental import pallas as pl
from jax.experimental.pallas import tpu as pltpu

F32 = jnp.float32
BF16 = jnp.bfloat16

MLA_HEADS = 8
QK_NOPE = 128
QK_ROPE = 64
V_HEAD = 128
Q_LORA = 512
KV_LORA = 512
ROPE_THETA = 10000.0
CONV_CH = 512
CONV_WIDTH = 31
HGRN_HEADS = 4
HGRN_DK = 128
HGRN_DV = 128
RMS_EPS = 1e-6
LN_EPS = 1e-5
QK_CAT = 256
V_ROWS = V_HEAD + 16
MAIN_COLS = 4096
CONV_HALO = 32

TM_IN = 1024
TR_IN = 512
TN_IN = 512
TS_QKV = 512
TS_CONV = 512
L_HGRN = 128
HGRN_STEP = 256
TM_OUT = 512
TR_OUT = 256
TM_UP = 2048
TF_UP = 512
TM_DOWN = 512
TN_DOWN = 512
MIB = 1024 * 1024


def _params(sem, vmem_mib):
    return pltpu.CompilerParams(dimension_semantics=sem, vmem_limit_bytes=vmem_mib * MIB)


def _rms_rows(x, w):
    ms = jnp.mean(x * x, axis=-1, keepdims=True)
    return x * lax.rsqrt(ms + RMS_EPS) * w


def _sigmoid(x):
    return 1.0 / (1.0 + jnp.exp(-x))


def _in_proj_body(x_ref, nw_ref, wm_ref, wk_ref, om_ref, ok_ref):
    for r in range(0, TM_IN, TR_IN):
        xn = _rms_rows(x_ref[r:r + TR_IN, :], nw_ref[...]).astype(BF16)
        for c in range(0, MAIN_COLS, TN_IN):
            om_ref[r:r + TR_IN, c:c + TN_IN] = jnp.dot(
                xn, wm_ref[:, c:c + TN_IN], preferred_element_type=F32).astype(BF16)
        ok_ref[r:r + TR_IN, :] = jnp.dot(xn, wk_ref[...], preferred_element_type=F32)


def _in_proj(h, norm_w, w_main, w_kr, l):
    t, d = h.shape
    once = pl.Buffered(1)
    return pl.pallas_call(
        _in_proj_body,
        grid=(t // TM_IN,),
        in_specs=[
            pl.BlockSpec((TM_IN, d), lambda i: (i, 0)),
            pl.BlockSpec((None, 1, d), lambda i: (l, 0, 0)),
            pl.BlockSpec((None, d, MAIN_COLS), lambda i: (l, 0, 0), pipeline_mode=once),
            pl.BlockSpec((None, d, 128), lambda i: (l, 0, 0), pipeline_mode=once),
        ],
        out_specs=[
            pl.BlockSpec((TM_IN, MAIN_COLS), lambda i: (i, 0)),
            pl.BlockSpec((TM_IN, 128), lambda i: (i, 0)),
        ],
        out_shape=[jax.ShapeDtypeStruct((t, MAIN_COLS), BF16), jax.ShapeDtypeStruct((t, 128), F32)],
        compiler_params=_params(("parallel",), 58),
        name="in_proj",
    )(h, norm_w, w_main, w_kr)


_NT = (((1,), (1,)), ((), ()))


def _q_proj_body(scale, c_ref, nw_ref, w_ref, cs_ref, o_ref):
    cn = _rms_rows(c_ref[...].astype(F32), nw_ref[...]).astype(BF16)
    cs = cs_ref[...] * scale
    for hd in range(MLA_HEADS):
        qt = lax.dot_general(w_ref[hd * QK_CAT:(hd + 1) * QK_CAT, :], cn, _NT,
                             preferred_element_type=F32)
        o_ref[hd, 0:QK_NOPE, :] = (qt[0:QK_NOPE, :] * scale).astype(BF16)
        o_ref[hd, QK_NOPE:QK_CAT, :] = (qt[QK_NOPE:QK_CAT, :] * cs).astype(BF16)


def _q_proj(proj_main, norm_w, w_uq_t, cs_t, l, b, s):
    ns = s // TS_QKV
    scale = (QK_NOPE + QK_ROPE) ** -0.5 * math.log2(math.e)
    return pl.pallas_call(
        functools.partial(_q_proj_body, scale),
        grid=(b, ns),
        in_specs=[
            pl.BlockSpec((TS_QKV, Q_LORA), lambda bi, i: (bi * ns + i, 0)),
            pl.BlockSpec((None, 1, Q_LORA), lambda bi, i: (l, 0, 0)),
            pl.BlockSpec((None, MLA_HEADS * QK_CAT, Q_LORA), lambda bi, i: (l, 0, 0)),
            pl.BlockSpec((128, TS_QKV), lambda bi, i: (0, bi * ns + i)),
        ],
        out_specs=pl.BlockSpec((None, MLA_HEADS, QK_CAT, TS_QKV), lambda bi, i: (bi, 0, 0, i)),
        out_shape=jax.ShapeDtypeStruct((b, MLA_HEADS, QK_CAT, s), BF16),
        compiler_params=_params(("parallel", "parallel"), 40),
        name="q_proj",
    )(proj_main, norm_w, w_uq_t, cs_t)


def _kv_proj_body(c_ref, kr_ref, nw_ref, wk_ref, wv_ref, cs_ref, k_ref, v_ref):
    cn = _rms_rows(c_ref[...].astype(F32), nw_ref[...]).astype(BF16)
    a = kr_ref[...] * cs_ref[...]
    krot = (a + pltpu.roll(a, 64, axis=1)).astype(BF16)
    ts = cn.shape[0]
    for h0 in range(0, MLA_HEADS, 2):
        kn = jnp.dot(cn, wk_ref[:, h0 * QK_NOPE:(h0 + 2) * QK_NOPE],
                     preferred_element_type=F32)
        vt = lax.dot_general(wv_ref[h0 * V_HEAD:(h0 + 2) * V_HEAD, :], cn, _NT,
                             preferred_element_type=F32)
        for d in range(2):
            hd = h0 + d
            k_ref[hd, :, 0:QK_NOPE] = kn[:, d * QK_NOPE:(d + 1) * QK_NOPE].astype(BF16)
            k_ref[hd, :, QK_NOPE:QK_CAT] = krot
            v_ref[hd, 0:V_HEAD, :] = vt[d * V_HEAD:(d + 1) * V_HEAD, :].astype(BF16)
            v_ref[hd, V_HEAD:V_ROWS, :] = jnp.ones((V_ROWS - V_HEAD, ts), BF16)


def _kv_proj(proj_main, kr, norm_w, w_k, w_v_t, cs, l, b, s):
    ns = s // TS_QKV
    return pl.pallas_call(
        _kv_proj_body,
        grid=(b, ns),
        in_specs=[
            pl.BlockSpec((TS_QKV, KV_LORA), lambda bi, i: (bi * ns + i, 1)),
            pl.BlockSpec((TS_QKV, 128), lambda bi, i: (bi * ns + i, 0)),
            pl.BlockSpec((None, 1, KV_LORA), lambda bi, i: (l, 0, 0)),
            pl.BlockSpec((None, KV_LORA, MLA_HEADS * QK_NOPE), lambda bi, i: (l, 0, 0)),
            pl.BlockSpec((None, MLA_HEADS * V_HEAD, KV_LORA), lambda bi, i: (l, 0, 0)),
            pl.BlockSpec((TS_QKV, 128), lambda bi, i: (bi * ns + i, 0)),
        ],
        out_specs=[
            pl.BlockSpec((None, MLA_HEADS, TS_QKV, QK_CAT), lambda bi, i: (bi, 0, i, 0)),
            pl.BlockSpec((None, MLA_HEADS, None, V_ROWS, TS_QKV), lambda bi, i: (bi, 0, i, 0, 0)),
        ],
        out_shape=[
            jax.ShapeDtypeStruct((b, MLA_HEADS, s, QK_CAT), BF16),
            jax.ShapeDtypeStruct((b, MLA_HEADS, ns, V_ROWS, TS_QKV), BF16),
        ],
        compiler_params=_params(("parallel", "parallel"), 40),
        name="kv_proj",
    )(proj_main, kr, norm_w, w_k, w_v_t, cs)


def _attn_body(tk, nq, q_ref, k_ref, v_ref, nw_ref, o_ref,
               sa_ref, sb_ref, xa_ref, xb_ref, m_ref, acc_ref):
    tq = 2 * tk
    key = lax.broadcasted_iota(jnp.int32, (tk, tq), 0)
    qry = lax.broadcasted_iota(jnp.int32, (tk, tq), 1)
    causal = key <= qry
    causal_half = causal[:, 0:tk]

    for qi in range(nq):
        q0 = qi * tq

        def fill(s_ref, x_ref, j, lo=0, hi=tq, mask=None):
            start = j * tk if isinstance(j, int) else pl.multiple_of(j * tk, tk)
            sc = jnp.dot(k_ref[pl.ds(start, tk), :], q_ref[:, q0 + lo:q0 + hi],
                         preferred_element_type=F32)
            if mask is not None:
                sc = jnp.where(mask, sc, -jnp.inf)
            s_ref[:, 0:hi - lo] = sc
            x_ref[:, 0:hi - lo] = jnp.max(sc, axis=0, keepdims=True)

        def update(s_ref, x_ref, j, lo=0, hi=tq):
            m = m_ref[:, lo:hi]
            m_new = jnp.maximum(m, x_ref[:, 0:hi - lo])
            p = jnp.exp2(s_ref[:, 0:hi - lo] - m_new)
            alpha = jnp.exp2(m - m_new)
            acc_ref[:, lo:hi] = alpha * acc_ref[:, lo:hi] + jnp.dot(
                v_ref[j], p.astype(BF16), preferred_element_type=F32)
            m_ref[:, lo:hi] = m_new

        m_ref[...] = jnp.full(m_ref.shape, -jnp.inf, F32)
        acc_ref[...] = jnp.zeros(acc_ref.shape, F32)
        first_mask = causal if qi == 0 else None
        fill(sa_ref, xa_ref, 0, mask=first_mask)

        def pair(jj):
            j0 = 2 * jj
            fill(sb_ref, xb_ref, j0 + 1)
            update(sa_ref, xa_ref, j0)
            fill(sa_ref, xa_ref, j0 + 2)
            update(sb_ref, xb_ref, j0 + 1)

        def two_pairs(it, carry):
            pair(2 * it)
            pair(2 * it + 1)
            return carry

        full = max(qi - 1, 0)
        if full >= 2:
            lax.fori_loop(0, full // 2, two_pairs, 0)
        if full % 2:
            pair(full - 1)
        if qi > 0:
            j0 = 2 * (qi - 1)
            fill(sb_ref, xb_ref, j0 + 1)
            update(sa_ref, xa_ref, j0)
            fill(sa_ref, xa_ref, j0 + 2, mask=causal)
            update(sb_ref, xb_ref, j0 + 1)
        fill(sb_ref, xb_ref, 2 * qi + 1, tk, tq, mask=causal_half)
        update(sa_ref, xa_ref, 2 * qi)
        update(sb_ref, xb_ref, 2 * qi + 1, tk, tq)
        o = (acc_ref[0:V_HEAD, :] / acc_ref[V_HEAD:V_HEAD + 1, :]).T
        o_ref[q0:q0 + tq, :] = _rms_rows(o, nw_ref[...]).astype(BF16)


def _attention(q_t, kcat, v_t, norm_w, l, b, s):
    tk = TS_QKV
    tq = 2 * tk
    nq = s // tq
    pitch = tq + 128
    return pl.pallas_call(
        functools.partial(_attn_body, tk, nq),
        grid=(b, MLA_HEADS),
        in_specs=[
            pl.BlockSpec((None, None, QK_CAT, s), lambda bi, hi: (bi, hi, 0, 0)),
            pl.BlockSpec((None, None, s, QK_CAT), lambda bi, hi: (bi, hi, 0, 0)),
            pl.BlockSpec((None, None, s // tk, V_ROWS, tk), lambda bi, hi: (bi, hi, 0, 0, 0)),
            pl.BlockSpec((None, 1, V_HEAD), lambda bi, hi: (l, 0, 0)),
        ],
        out_specs=pl.BlockSpec((s, V_HEAD), lambda bi, hi: (bi, hi)),
        out_shape=jax.ShapeDtypeStruct((b * s, MLA_HEADS * V_HEAD), BF16),
        scratch_shapes=[
            pltpu.VMEM((tk, pitch), F32), pltpu.VMEM((tk, pitch), F32),
            pltpu.VMEM((1, tq), F32), pltpu.VMEM((1, tq), F32),
            pltpu.VMEM((1, tq), F32), pltpu.VMEM((V_ROWS, tq), F32),
        ],
        compiler_params=_params(("parallel", "parallel"), 48),
        name="mla_attention",
    )(q_t, kcat, v_t, norm_w)


def _conv_body(ts, u_ref, halo_ref, cw_ref, cb_ref, lw_ref, lb_ref, o_ref, hp_ref):
    i = pl.program_id(1)

    def glu(u):
        u = u.astype(F32)
        return u[:, 0:CONV_CH] * _sigmoid(u[:, CONV_CH:2 * CONV_CH])

    hp_ref[CONV_HALO:CONV_HALO + ts, :] = glu(u_ref[...])
    hp_ref[0:CONV_HALO, :] = jnp.where(i == 0, 0.0, glu(halo_ref[...]))
    off = CONV_HALO - (CONV_WIDTH - 1)
    acc = jnp.zeros((ts, CONV_CH), F32)
    hp = hp_ref[...]
    rows = hp.shape[0]
    for r in range(8):
        last = (CONV_WIDTH - 1 - r) // 8
        xr = pltpu.roll(hp, rows - (off + r), axis=0)
        for g in range(last + 1):
            acc = acc + xr[8 * g:8 * g + ts, :] * cw_ref[8 * g + r:8 * g + r + 1, :]
    acc = acc + cb_ref[...]
    mu = jnp.mean(acc, axis=-1, keepdims=True)
    xc = acc - mu
    var = jnp.mean(xc * xc, axis=-1, keepdims=True)
    y = xc * lax.rsqrt(var + LN_EPS) * lw_ref[...] + lb_ref[...]
    o_ref[...] = (y * _sigmoid(y)).astype(BF16)


def _conv(proj_main, conv_w, conv_b, ln_w, ln_b, l, b, s):
    ns = s // TS_CONV
    per = TS_CONV // CONV_HALO

    def halo_map(bi, i):
        return (jnp.maximum((bi * ns + i) * per - 1, 0), 1)

    return pl.pallas_call(
        functools.partial(_conv_body, TS_CONV),
        grid=(b, ns),
        in_specs=[
            pl.BlockSpec((TS_CONV, 2 * CONV_CH), lambda bi, i: (bi * ns + i, 1)),
            pl.BlockSpec((CONV_HALO, 2 * CONV_CH), halo_map),
            pl.BlockSpec((None, CONV_WIDTH, CONV_CH), lambda bi, i: (l, 0, 0)),
            pl.BlockSpec((None, 1, CONV_CH), lambda bi, i: (l, 0, 0)),
            pl.BlockSpec((None, 1, CONV_CH), lambda bi, i: (l, 0, 0)),
            pl.BlockSpec((None, 1, CONV_CH), lambda bi, i: (l, 0, 0)),
        ],
        out_specs=pl.BlockSpec((TS_CONV, CONV_CH), lambda bi, i: (bi * ns + i, 0)),
        out_shape=jax.ShapeDtypeStruct((b * s, CONV_CH), BF16),
        scratch_shapes=[pltpu.VMEM((CONV_HALO + TS_CONV, CONV_CH), F32)],
        compiler_params=_params(("parallel", "parallel"), 32),
        name="conv_module",
    )(proj_main, proj_main, conv_w, conv_b, ln_w, ln_b)


def _hgrn_tables(tile):
    levels = int(math.log2(tile))
    t = np.arange(tile)
    tri = (t[None, :] <= t[:, None]).astype(np.float32)
    mats = [tri]
    masks = [np.eye(tile, dtype=np.float32)]
    for j in range(levels):
        n = 1 << j
        t0 = (t & ~(2 * n - 1)) + n - 1
        diff = tri - (t[None, :] <= t0[:, None]).astype(np.float32)
        is_lower = (((t >> j) & 1) == 1)[:, None]
        mats.append(np.where(is_lower, diff, -diff))
        same_block = (t[:, None] >> (j + 1)) == (t[None, :] >> (j + 1))
        lower = ((t[:, None] >> j) & 1) == 1
        upper = ((t[None, :] >> j) & 1) == 0
        masks.append((same_block & lower & upper).astype(np.float32))
    stacked = np.concatenate(mats, axis=0)
    return levels, np.concatenate([stacked, stacked], axis=1), np.stack(masks, axis=0)


def _hgrn_body(tile, levels, hq_ref, hf_ref, hi_ref, hg_ref, lb_ref, nw_ref, mat_ref, mask_ref,
               o_ref, st_ref):
    i = pl.program_id(1)

    @pl.when(i == 0)
    def _():
        st_ref[...] = jnp.zeros_like(st_ref)

    for r0 in range(0, hq_ref.shape[0], tile):
        _hgrn_tile(tile, levels, slice(r0, r0 + tile), hq_ref, hf_ref, hi_ref, hg_ref, lb_ref, nw_ref,
                   mat_ref, mask_ref, o_ref, st_ref)


def _hgrn_tile(tile, levels, rows, hq_ref, hf_ref, hi_ref, hg_ref, lb_ref, nw_ref, mat_ref, mask_ref,
               o_ref, st_ref):
    heads = [slice(hd * HGRN_DK, (hd + 1) * HGRN_DK) for hd in range(HGRN_HEADS)]
    lb = lb_ref[...]
    z = hf_ref[rows, :].astype(F32)
    la = jnp.log(lb)
    lc = jnp.log1p(-lb) + (jnp.minimum(z, 0.0) - jnp.log1p(jnp.exp(-jnp.abs(z))))
    logf = jnp.maximum(la, lc) + jnp.log1p(jnp.exp(-jnp.abs(la - lc)))
    kk = (1.0 - lb) / (1.0 + jnp.exp(z))
    xq = hq_ref[rows, :].astype(F32)
    q = xq * _sigmoid(xq)
    xg = hg_ref[rows, :].astype(F32)
    g = xg * _sigmoid(xg)
    v = hi_ref[rows, :]

    lf2 = logf * math.log2(math.e)
    lf_hi = lf2.astype(BF16)
    lf_lo = (lf2 - lf_hi.astype(F32)).astype(BF16)
    dall = jnp.dot(mat_ref[...], jnp.concatenate([lf_hi, lf_lo], axis=0),
                   preferred_element_type=F32)
    bcum = dall[0:tile]
    blast = bcum[tile - 1:tile, :]
    q_b = q.astype(BF16)
    k_b = kk.astype(BF16)

    a = [mask_ref[0] * lax.dot_general(q_b[:, c], k_b[:, c], _NT, preferred_element_type=F32)
         for c in heads]
    for j in range(levels):
        wj = jnp.exp2(dall[(j + 1) * tile:(j + 2) * tile])
        qw = (q * wj).astype(BF16)
        kw = (kk * wj).astype(BF16)
        mj = mask_ref[j + 1]
        a = [a[hd] + mj * lax.dot_general(qw[:, c], kw[:, c], _NT, preferred_element_type=F32)
             for hd, c in enumerate(heads)]
    q_dec = (q * jnp.exp2(bcum)).astype(BF16)
    k_dec = (kk * jnp.exp2(blast - bcum)).astype(BF16)
    s_dec = jnp.exp2(blast)
    for hd, c in enumerate(heads):
        st = st_ref[hd]
        o = jnp.dot(a[hd].astype(BF16), v[:, c], preferred_element_type=F32)
        o = o + lax.dot_general(q_dec[:, c], st.astype(BF16), _NT, preferred_element_type=F32)
        st_ref[hd] = st * s_dec[:, c] + lax.dot_general(
            v[:, c], k_dec[:, c], (((0,), (0,)), ((), ())), preferred_element_type=F32)
        o_ref[rows, c] = (_rms_rows(o, nw_ref[...]) * g[:, c]).astype(BF16)


def _hgrn(proj_main, lb, norm_w, l, b, s):
    tile = L_HGRN
    step = HGRN_STEP
    nt = s // step
    levels, mats, masks = _hgrn_tables(tile)
    width = HGRN_HEADS * HGRN_DK
    row = lambda bi, i: bi * nt + i
    return pl.pallas_call(
        functools.partial(_hgrn_body, tile, levels),
        grid=(b, nt),
        in_specs=[
            pl.BlockSpec((step, width), lambda bi, i: (row(bi, i), 4)),
            pl.BlockSpec((step, width), lambda bi, i: (row(bi, i), 5)),
            pl.BlockSpec((step, width), lambda bi, i: (row(bi, i), 6)),
            pl.BlockSpec((step, width), lambda bi, i: (row(bi, i), 7)),
            pl.BlockSpec((None, 1, width), lambda bi, i: (l, 0, 0)),
            pl.BlockSpec((None, 1, HGRN_DV), lambda bi, i: (l, 0, 0)),
            pl.BlockSpec(((levels + 1) * tile, 2 * tile), lambda bi, i: (0, 0)),
            pl.BlockSpec((levels + 1, tile, tile), lambda bi, i: (0, 0, 0)),
        ],
        out_specs=pl.BlockSpec((step, width), lambda bi, i: (row(bi, i), 0)),
        out_shape=jax.ShapeDtypeStruct((b * s, width), BF16),
        scratch_shapes=[pltpu.VMEM((HGRN_HEADS, HGRN_DV, HGRN_DK), F32)],
        compiler_params=_params(("parallel", "arbitrary"), 32),
        name="hgrn2",
    )(proj_main, proj_main, proj_main, proj_main, lb, norm_w,
      jnp.asarray(mats, BF16), jnp.asarray(masks, F32))


def _out_proj_body(ya_ref, yb_ref, yc_ref, w_ref, h_ref, nw_ref, o_ref, xn_ref):
    na = ya_ref.shape[1]
    nb = na + yb_ref.shape[1]
    for r in range(0, TM_OUT, TR_OUT):
        rows = slice(r, r + TR_OUT)
        acc = jnp.dot(ya_ref[rows, :], w_ref[0:na, :], preferred_element_type=F32)
        acc = acc + jnp.dot(yb_ref[rows, :], w_ref[na:nb, :], preferred_element_type=F32)
        acc = acc + jnp.dot(yc_ref[rows, :], w_ref[nb:, :], preferred_element_type=F32)
        hn = h_ref[rows, :] + acc
        o_ref[rows, :] = hn
        xn_ref[rows, :] = _rms_rows(hn, nw_ref[...]).astype(BF16)


def _out_proj(ya, yb, yc, w_out, h, ffn_norm_w, l):
    t, d = h.shape
    dm = w_out.shape[1]
    return pl.pallas_call(
        _out_proj_body,
        grid=(t // TM_OUT,),
        in_specs=[
            pl.BlockSpec((TM_OUT, ya.shape[1]), lambda i: (i, 0)),
            pl.BlockSpec((TM_OUT, yb.shape[1]), lambda i: (i, 0)),
            pl.BlockSpec((TM_OUT, yc.shape[1]), lambda i: (i, 0)),
            pl.BlockSpec((None, dm, d), lambda i: (l, 0, 0), pipeline_mode=pl.Buffered(1)),
            pl.BlockSpec((TM_OUT, d), lambda i: (i, 0)),
            pl.BlockSpec((None, 1, d), lambda i: (l, 0, 0)),
        ],
        out_specs=[pl.BlockSpec((TM_OUT, d), lambda i: (i, 0)), pl.BlockSpec((TM_OUT, d), lambda i: (i, 0))],
        out_shape=[jax.ShapeDtypeStruct((t, d), F32), jax.ShapeDtypeStruct((t, d), BF16)],
        compiler_params=_params(("parallel",), 56),
        name="out_proj",
    )(ya, yb, yc, w_out, h, ffn_norm_w)


def _ffn_up_body(xn_ref, wg_ref, wu_ref, o_ref):
    xn = xn_ref[...]
    for c in range(0, TF_UP, 256):
        gate = jnp.dot(xn, wg_ref[:, c:c + 256], preferred_element_type=F32)
        up = jnp.dot(xn, wu_ref[:, c:c + 256], preferred_element_type=F32)
        o_ref[:, c:c + 256] = (gate * _sigmoid(gate) * up).astype(BF16)


def _ffn_up(xn, w_gate, w_up, l):
    t, d = xn.shape
    f = w_gate.shape[2]
    return pl.pallas_call(
        _ffn_up_body,
        grid=(t // TM_UP, f // TF_UP),
        in_specs=[
            pl.BlockSpec((TM_UP, d), lambda i, j: (i, 0)),
            pl.BlockSpec((None, d, TF_UP), lambda i, j: (l, 0, j)),
            pl.BlockSpec((None, d, TF_UP), lambda i, j: (l, 0, j)),
        ],
        out_specs=pl.BlockSpec((TM_UP, TF_UP), lambda i, j: (i, j)),
        out_shape=jax.ShapeDtypeStruct((t, f), BF16),
        compiler_params=_params(("parallel", "arbitrary"), 40),
        name="ffn_up",
    )(xn, w_gate, w_up)


def _ffn_down_body(last, a_ref, w_ref, h_ref, fw_ref, o_ref):
    d = o_ref.shape[1]
    for c in range(0, d, TN_DOWN):
        o_ref[:, c:c + TN_DOWN] = h_ref[:, c:c + TN_DOWN] + jnp.dot(
            a_ref[...], w_ref[:, c:c + TN_DOWN], preferred_element_type=F32)
    if last:
        o_ref[...] = _rms_rows(o_ref[...], fw_ref[...])


def _ffn_down(ff, w_down, h, final_w, l, last):
    t, d = h.shape
    f = ff.shape[1]
    return pl.pallas_call(
        functools.partial(_ffn_down_body, last),
        grid=(t // TM_DOWN,),
        in_specs=[
            pl.BlockSpec((TM_DOWN, f), lambda i: (i, 0)),
            pl.BlockSpec((None, f, d), lambda i: (l, 0, 0), pipeline_mode=pl.Buffered(1)),
            pl.BlockSpec((TM_DOWN, d), lambda i: (i, 0)),
            pl.BlockSpec((1, d), lambda i: (0, 0)),
        ],
        out_specs=pl.BlockSpec((TM_DOWN, d), lambda i: (i, 0)),
        out_shape=jax.ShapeDtypeStruct((t, d), F32),
        compiler_params=_params(("parallel",), 58),
        name="ffn_down",
    )(ff, w_down, h, final_w)


def _rope_swap(cols):
    half = QK_ROPE // 2
    return np.concatenate([cols[half:], cols[:half]])


def _w_in_columns():
    kr0 = Q_LORA + KV_LORA
    main = np.concatenate([np.arange(0, kr0), np.arange(kr0 + QK_ROPE, kr0 + QK_ROPE + MAIN_COLS - kr0)])
    kr = np.arange(kr0, kr0 + QK_ROPE)
    return main, np.concatenate([kr, _rope_swap(kr)])


def _w_uq_columns():
    per = QK_NOPE + QK_ROPE
    out = []
    for hd in range(MLA_HEADS):
        pe = np.arange(hd * per + QK_NOPE, (hd + 1) * per)
        out += [np.arange(hd * per, hd * per + QK_NOPE), pe, _rope_swap(pe)]
    return np.concatenate(out)


def _w_ukv_columns():
    per = QK_NOPE + V_HEAD
    kn = [np.arange(hd * per, hd * per + QK_NOPE) for hd in range(MLA_HEADS)]
    vv = [np.arange(hd * per + QK_NOPE, (hd + 1) * per) for hd in range(MLA_HEADS)]
    return np.concatenate(kn), np.concatenate(vv)


@jax.jit
def _trunk(x, positions, attn_norm_w, w_in, q_norm_w, w_uq, kv_norm_w, w_ukv, mla_out_norm_w,
           conv_w, conv_b, conv_ln_w, conv_ln_b, hgrn_lower_bounds, hgrn_norm_w, w_out,
           ffn_norm_w, w_gate, w_up, w_down, final_norm_w):
    b, s, d = x.shape
    depth = w_in.shape[0]
    t = b * s
    for rows in (TM_IN, TM_OUT, TM_UP, TM_DOWN):
        assert t % rows == 0, (t, rows)
    for rows in (2 * TS_QKV, TS_CONV, HGRN_STEP):
        assert s % rows == 0, (s, rows)

    inv_freq = ROPE_THETA ** (-jnp.arange(0, QK_ROPE, 2, dtype=F32) / QK_ROPE)
    ang = positions.astype(F32)[..., None] * inv_freq
    cos, sin = jnp.cos(ang), jnp.sin(ang)
    cs = jnp.concatenate([cos, cos, -sin, sin], axis=-1).reshape(t, 2 * QK_ROPE)
    cs_t = cs.T
    lb_all = jnp.cumsum(jax.nn.softmax(hgrn_lower_bounds.astype(F32), axis=0), axis=0)
    lb_all = (lb_all - lb_all[0:1])[:, None, :]

    main_cols, kr_cols = _w_in_columns()
    kr0 = int(kr_cols[0])
    assert np.array_equal(main_cols, np.r_[0:kr0, kr0 + QK_ROPE:w_in.shape[2]])
    w_main = jnp.concatenate([w_in[:, :, :kr0], w_in[:, :, kr0 + QK_ROPE:]], axis=2).astype(BF16)
    w_kr = w_in[:, :, kr_cols].astype(BF16)
    w_uq_t = jnp.swapaxes(w_uq[:, :, _w_uq_columns()], 1, 2).astype(BF16)
    kn_cols, v_cols = _w_ukv_columns()
    w_k_b = w_ukv[:, :, kn_cols].astype(BF16)
    w_v_t = jnp.swapaxes(w_ukv[:, :, v_cols], 1, 2).astype(BF16)
    w_out_b = w_out.astype(BF16)
    w_gate_b = w_gate.astype(BF16)
    w_up_b = w_up.astype(BF16)
    w_down_b = w_down.astype(BF16)
    row3 = lambda a: a[:, None, :]

    h = x.reshape(t, d)
    for l in range(depth):
        proj_main, kr = _in_proj(h, row3(attn_norm_w), w_main, w_kr, l)
        q_t = _q_proj(proj_main, row3(q_norm_w), w_uq_t, cs_t, l, b, s)
        kcat, v_t = _kv_proj(proj_main, kr, row3(kv_norm_w), w_k_b, w_v_t, cs, l, b, s)
        ya = _attention(q_t, kcat, v_t, row3(mla_out_norm_w), l, b, s)
        yb = _conv(proj_main, conv_w, row3(conv_b), row3(conv_ln_w), row3(conv_ln_b), l, b, s)
        yc = _hgrn(proj_main, lb_all, row3(hgrn_norm_w), l, b, s)
        h, xn = _out_proj(ya, yb, yc, w_out_b, h, row3(ffn_norm_w), l)
        ff = _ffn_up(xn, w_gate_b, w_up_b, l)
        h = _ffn_down(ff, w_down_b, h, final_norm_w[None, :], l, last=(l == depth - 1))
    return h.reshape(b, s, d)


def kernel(x, positions, attn_norm_w, w_in, q_norm_w, w_uq, kv_norm_w, w_ukv, mla_out_norm_w,
           conv_w, conv_b, conv_ln_w, conv_ln_b, hgrn_lower_bounds, hgrn_norm_w, w_out,
           ffn_norm_w, w_gate, w_up, w_down, final_norm_w):
    return _trunk(x, positions, attn_norm_w, w_in, q_norm_w, w_uq, kv_norm_w, w_ukv, mla_out_norm_w,
                  conv_w, conv_b, conv_ln_w, conv_ln_b, hgrn_lower_bounds, hgrn_norm_w, w_out,
                  ffn_norm_w, w_gate, w_up, w_down, final_norm_w)
```

```python
import functools
import math

import jax
import jax.numpy as jnp
import numpy as np
from jax import lax
from jax.experimental import pallas as pl
from jax.experimental.pallas import tpu as pltpu

F32 = jnp.float32
BF16 = jnp.bfloat16

MLA_HEADS = 8
QK_NOPE = 128
QK_ROPE = 64
V_HEAD = 128
Q_LORA = 512
KV_LORA = 512
ROPE_THETA = 10000.0
CONV_CH = 512
CONV_WIDTH = 31
HGRN_HEADS = 4
HGRN_DK = 128
HGRN_DV = 128
RMS_EPS = 1e-6
LN_EPS = 1e-5
QK_CAT = 256
V_ROWS = V_HEAD + 16
MAIN_COLS = 4096
CONV_HALO = 32

TM_IN = 1024
TR_IN = 512
TN_IN = 512
TS_QKV = 512
TS_CONV = 512
L_HGRN = 128
HGRN_STEP = 256
TM_OUT = 512
TR_OUT = 256
TM_UP = 2048
TR_UP = 512
TF_UP = 512
TM_DOWN = 512
TN_DOWN = 512
MIB = 1024 * 1024


def _params(sem, vmem_mib):
    return pltpu.CompilerParams(dimension_semantics=sem, vmem_limit_bytes=vmem_mib * MIB)


def _rms_rows(x, w):
    ms = jnp.mean(x * x, axis=-1, keepdims=True)
    return x * lax.rsqrt(ms + RMS_EPS) * w


def _sigmoid(x):
    return 1.0 / (1.0 + jnp.exp(-x))


def _in_proj_body(x_ref, nw_ref, wm_ref, wk_ref, om_ref, ok_ref):
    for r in range(0, TM_IN, TR_IN):
        xn = _rms_rows(x_ref[r:r + TR_IN, :], nw_ref[...]).astype(BF16)
        for c in range(0, MAIN_COLS, TN_IN):
            om_ref[r:r + TR_IN, c:c + TN_IN] = jnp.dot(
                xn, wm_ref[:, c:c + TN_IN], preferred_element_type=F32).astype(BF16)
        ok_ref[r:r + TR_IN, :] = jnp.dot(xn, wk_ref[...], preferred_element_type=F32)


def _in_proj(h, norm_w, w_main, w_kr, l):
    t, d = h.shape
    once = pl.Buffered(1)
    return pl.pallas_call(
        _in_proj_body,
        grid=(t // TM_IN,),
        in_specs=[
            pl.BlockSpec((TM_IN, d), lambda i: (i, 0)),
            pl.BlockSpec((None, 1, d), lambda i: (l, 0, 0)),
            pl.BlockSpec((None, d, MAIN_COLS), lambda i: (l, 0, 0), pipeline_mode=once),
            pl.BlockSpec((None, d, 128), lambda i: (l, 0, 0), pipeline_mode=once),
        ],
        out_specs=[
            pl.BlockSpec((TM_IN, MAIN_COLS), lambda i: (i, 0)),
            pl.BlockSpec((TM_IN, 128), lambda i: (i, 0)),
        ],
        out_shape=[jax.ShapeDtypeStruct((t, MAIN_COLS), BF16), jax.ShapeDtypeStruct((t, 128), F32)],
        compiler_params=_params(("parallel",), 58),
        name="in_proj",
    )(h, norm_w, w_main, w_kr)


_NT = (((1,), (1,)), ((), ()))


def _q_proj_body(scale, c_ref, nw_ref, w_ref, cs_ref, o_ref):
    cn = _rms_rows(c_ref[...].astype(F32), nw_ref[...]).astype(BF16)
    cs = cs_ref[...] * scale
    for hd in range(MLA_HEADS):
        qt = lax.dot_general(w_ref[hd * QK_CAT:(hd + 1) * QK_CAT, :], cn, _NT,
                             preferred_element_type=F32)
        o_ref[hd, 0:QK_NOPE, :] = (qt[0:QK_NOPE, :] * scale).astype(BF16)
        o_ref[hd, QK_NOPE:QK_CAT, :] = (qt[QK_NOPE:QK_CAT, :] * cs).astype(BF16)


def _q_proj(proj_main, norm_w, w_uq_t, cs_t, l, b, s):
    tq = 2 * TS_QKV
    ns = s // tq
    scale = (QK_NOPE + QK_ROPE) ** -0.5 * math.log2(math.e)
    return pl.pallas_call(
        functools.partial(_q_proj_body, scale),
        grid=(b, ns),
        in_specs=[
            pl.BlockSpec((tq, Q_LORA), lambda bi, i: (bi * ns + i, 0)),
            pl.BlockSpec((None, 1, Q_LORA), lambda bi, i: (l, 0, 0)),
            pl.BlockSpec((None, MLA_HEADS * QK_CAT, Q_LORA), lambda bi, i: (l, 0, 0)),
            pl.BlockSpec((128, tq), lambda bi, i: (0, bi * ns + i)),
        ],
        out_specs=pl.BlockSpec((None, MLA_HEADS, None, QK_CAT, tq), lambda bi, i: (bi, 0, i, 0, 0)),
        out_shape=jax.ShapeDtypeStruct((b, MLA_HEADS, ns, QK_CAT, tq), BF16),
        compiler_params=_params(("parallel", "parallel"), 40),
        name="q_proj",
    )(proj_main, norm_w, w_uq_t, cs_t)


def _kv_proj_body(c_ref, kr_ref, nw_ref, wk_ref, wv_ref, cs_ref, k_ref, v_ref):
    cn = _rms_rows(c_ref[...].astype(F32), nw_ref[...]).astype(BF16)
    a = kr_ref[...] * cs_ref[...]
    krot = (a + pltpu.roll(a, 64, axis=1)).astype(BF16)
    ts = cn.shape[0]
    for h0 in range(0, MLA_HEADS, 2):
        kn = jnp.dot(cn, wk_ref[:, h0 * QK_NOPE:(h0 + 2) * QK_NOPE],
                     preferred_element_type=F32)
        vt = lax.dot_general(wv_ref[h0 * V_HEAD:(h0 + 2) * V_HEAD, :], cn, _NT,
                             preferred_element_type=F32)
        for d in range(2):
            hd = h0 + d
            k_ref[hd, :, 0:QK_NOPE] = kn[:, d * QK_NOPE:(d + 1) * QK_NOPE].astype(BF16)
            k_ref[hd, :, QK_NOPE:QK_CAT] = krot
            v_ref[hd, 0:V_HEAD, :] = vt[d * V_HEAD:(d + 1) * V_HEAD, :].astype(BF16)
            v_ref[hd, V_HEAD:V_ROWS, :] = jnp.ones((V_ROWS - V_HEAD, ts), BF16)


def _kv_proj(proj_main, kr, norm_w, w_k, w_v_t, cs, l, b, s):
    ns = s // TS_QKV
    return pl.pallas_call(
        _kv_proj_body,
        grid=(b, ns),
        in_specs=[
            pl.BlockSpec((TS_QKV, KV_LORA), lambda bi, i: (bi * ns + i, 1)),
            pl.BlockSpec((TS_QKV, 128), lambda bi, i: (bi * ns + i, 0)),
            pl.BlockSpec((None, 1, KV_LORA), lambda bi, i: (l, 0, 0)),
            pl.BlockSpec((None, KV_LORA, MLA_HEADS * QK_NOPE), lambda bi, i: (l, 0, 0)),
            pl.BlockSpec((None, MLA_HEADS * V_HEAD, KV_LORA), lambda bi, i: (l, 0, 0)),
            pl.BlockSpec((TS_QKV, 128), lambda bi, i: (bi * ns + i, 0)),
        ],
        out_specs=[
            pl.BlockSpec((None, MLA_HEADS, TS_QKV, QK_CAT), lambda bi, i: (bi, 0, i, 0)),
            pl.BlockSpec((None, MLA_HEADS, None, V_ROWS, TS_QKV), lambda bi, i: (bi, 0, i, 0, 0)),
        ],
        out_shape=[
            jax.ShapeDtypeStruct((b, MLA_HEADS, s, QK_CAT), BF16),
            jax.ShapeDtypeStruct((b, MLA_HEADS, ns, V_ROWS, TS_QKV), BF16),
        ],
        compiler_params=_params(("parallel", "parallel"), 40),
        name="kv_proj",
    )(proj_main, kr, norm_w, w_k, w_v_t, cs)


def _attn_body(tk, nq, q_ref, k_ref, v_ref, nw_ref, o_ref,
               sa_ref, sb_ref, xa_ref, xb_ref, m_ref, acc_ref):
    tq = 2 * tk
    key = lax.broadcasted_iota(jnp.int32, (tk, tq), 0)
    qry = lax.broadcasted_iota(jnp.int32, (tk, tq), 1)
    causal = key <= qry
    causal_half = causal[:, 0:tk]

    for qi in range(nq):
        q0 = qi * tq

        def fill(s_ref, x_ref, j, lo=0, hi=tq, mask=None):
            start = j * tk if isinstance(j, int) else pl.multiple_of(j * tk, tk)
            sc = jnp.dot(k_ref[pl.ds(start, tk), :], q_ref[qi, :, lo:hi],
                         preferred_element_type=F32)
            if mask is not None:
                sc = jnp.where(mask, sc, -jnp.inf)
            s_ref[:, 0:hi - lo] = sc
            x_ref[:, 0:hi - lo] = jnp.max(sc, axis=0, keepdims=True)

        def update(s_ref, x_ref, j, lo=0, hi=tq):
            m = m_ref[:, lo:hi]
            m_new = jnp.maximum(m, x_ref[:, 0:hi - lo])
            p = jnp.exp2(s_ref[:, 0:hi - lo] - m_new)
            alpha = jnp.exp2(m - m_new)
            acc_ref[:, lo:hi] = alpha * acc_ref[:, lo:hi] + jnp.dot(
                v_ref[j], p.astype(BF16), preferred_element_type=F32)
            m_ref[:, lo:hi] = m_new

        m_ref[...] = jnp.full(m_ref.shape, -jnp.inf, F32)
        acc_ref[...] = jnp.zeros(acc_ref.shape, F32)
        first_mask = causal if qi == 0 else None
        fill(sa_ref, xa_ref, 0, mask=first_mask)

        def pair(jj):
            j0 = 2 * jj
            fill(sb_ref, xb_ref, j0 + 1)
            update(sa_ref, xa_ref, j0)
            fill(sa_ref, xa_ref, j0 + 2)
            update(sb_ref, xb_ref, j0 + 1)

        def two_pairs(it, carry):
            pair(2 * it)
            pair(2 * it + 1)
            return carry

        full = max(qi - 1, 0)
        if full >= 2:
            lax.fori_loop(0, full // 2, two_pairs, 0)
        if full % 2:
            pair(full - 1)
        if qi > 0:
            j0 = 2 * (qi - 1)
            fill(sb_ref, xb_ref, j0 + 1)
            update(sa_ref, xa_ref, j0)
            fill(sa_ref, xa_ref, j0 + 2, mask=causal)
            update(sb_ref, xb_ref, j0 + 1)
        fill(sb_ref, xb_ref, 2 * qi + 1, tk, tq, mask=causal_half)
        update(sa_ref, xa_ref, 2 * qi)
        update(sb_ref, xb_ref, 2 * qi + 1, tk, tq)
        o = (acc_ref[0:V_HEAD, :] / acc_ref[V_HEAD:V_HEAD + 1, :]).T
        o_ref[q0:q0 + tq, :] = _rms_rows(o, nw_ref[...]).astype(BF16)


def _attention(q_t, kcat, v_t, norm_w, l, b, s):
    tk = TS_QKV
    tq = 2 * tk
    nq = s // tq
    pitch = tq + 128
    return pl.pallas_call(
        functools.partial(_attn_body, tk, nq),
        grid=(b, MLA_HEADS),
        in_specs=[
            pl.BlockSpec((None, None, nq, QK_CAT, tq), lambda bi, hi: (bi, hi, 0, 0, 0)),
            pl.BlockSpec((None, None, s, QK_CAT), lambda bi, hi: (bi, hi, 0, 0)),
            pl.BlockSpec((None, None, s // tk, V_ROWS, tk), lambda bi, hi: (bi, hi, 0, 0, 0)),
            pl.BlockSpec((None, 1, V_HEAD), lambda bi, hi: (l, 0, 0)),
        ],
        out_specs=pl.BlockSpec((s, V_HEAD), lambda bi, hi: (bi, hi)),
        out_shape=jax.ShapeDtypeStruct((b * s, MLA_HEADS * V_HEAD), BF16),
        scratch_shapes=[
            pltpu.VMEM((tk, pitch), F32), pltpu.VMEM((tk, pitch), F32),
            pltpu.VMEM((1, tq), F32), pltpu.VMEM((1, tq), F32),
            pltpu.VMEM((1, tq), F32), pltpu.VMEM((V_ROWS, tq), F32),
        ],
        compiler_params=_params(("parallel", "parallel"), 48),
        name="mla_attention",
    )(q_t, kcat, v_t, norm_w)


def _conv_body(ts, u_ref, halo_ref, cw_ref, cb_ref, lw_ref, lb_ref, o_ref, hp_ref):
    i = pl.program_id(1)

    def glu(u):
        u = u.astype(F32)
        return u[:, 0:CONV_CH] * _sigmoid(u[:, CONV_CH:2 * CONV_CH])

    hp_ref[CONV_HALO:CONV_HALO + ts, :] = glu(u_ref[...])
    hp_ref[0:CONV_HALO, :] = jnp.where(i == 0, 0.0, glu(halo_ref[...]))
    off = CONV_HALO - (CONV_WIDTH - 1)
    acc = jnp.zeros((ts, CONV_CH), F32)
    hp = hp_ref[...]
    rows = hp.shape[0]
    for r in range(8):
        last = (CONV_WIDTH - 1 - r) // 8
        xr = pltpu.roll(hp, rows - (off + r), axis=0)
        for g in range(last + 1):
            acc = acc + xr[8 * g:8 * g + ts, :] * cw_ref[8 * g + r:8 * g + r + 1, :]
    acc = acc + cb_ref[...]
    mu = jnp.mean(acc, axis=-1, keepdims=True)
    xc = acc - mu
    var = jnp.mean(xc * xc, axis=-1, keepdims=True)
    y = xc * lax.rsqrt(var + LN_EPS) * lw_ref[...] + lb_ref[...]
    o_ref[...] = (y * _sigmoid(y)).astype(BF16)


def _conv(proj_main, conv_w, conv_b, ln_w, ln_b, l, b, s):
    ns = s // TS_CONV
    per = TS_CONV // CONV_HALO

    def halo_map(bi, i):
        return (jnp.maximum((bi * ns + i) * per - 1, 0), 1)

    return pl.pallas_call(
        functools.partial(_conv_body, TS_CONV),
        grid=(b, ns),
        in_specs=[
            pl.BlockSpec((TS_CONV, 2 * CONV_CH), lambda bi, i: (bi * ns + i, 1)),
            pl.BlockSpec((CONV_HALO, 2 * CONV_CH), halo_map),
            pl.BlockSpec((None, CONV_WIDTH, CONV_CH), lambda bi, i: (l, 0, 0)),
            pl.BlockSpec((None, 1, CONV_CH), lambda bi, i: (l, 0, 0)),
            pl.BlockSpec((None, 1, CONV_CH), lambda bi, i: (l, 0, 0)),
            pl.BlockSpec((None, 1, CONV_CH), lambda bi, i: (l, 0, 0)),
        ],
        out_specs=pl.BlockSpec((TS_CONV, CONV_CH), lambda bi, i: (bi * ns + i, 0)),
        out_shape=jax.ShapeDtypeStruct((b * s, CONV_CH), BF16),
        scratch_shapes=[pltpu.VMEM((CONV_HALO + TS_CONV, CONV_CH), F32)],
        compiler_params=_params(("parallel", "parallel"), 32),
        name="conv_module",
    )(proj_main, proj_main, conv_w, conv_b, ln_w, ln_b)


def _hgrn_tables(tile):
    levels = int(math.log2(tile))
    t = np.arange(tile)
    tri = (t[None, :] <= t[:, None]).astype(np.float32)
    mats = [tri]
    masks = [np.eye(tile, dtype=np.float32)]
    for j in range(levels):
        n = 1 << j
        t0 = (t & ~(2 * n - 1)) + n - 1
        diff = tri - (t[None, :] <= t0[:, None]).astype(np.float32)
        is_lower = (((t >> j) & 1) == 1)[:, None]
        mats.append(np.where(is_lower, diff, -diff))
        same_block = (t[:, None] >> (j + 1)) == (t[None, :] >> (j + 1))
        lower = ((t[:, None] >> j) & 1) == 1
        upper = ((t[None, :] >> j) & 1) == 0
        masks.append((same_block & lower & upper).astype(np.float32))
    stacked = np.concatenate(mats, axis=0)
    return levels, np.concatenate([stacked, stacked], axis=1), np.stack(masks, axis=0)


def _hgrn_body(tile, levels, hq_ref, hf_ref, hi_ref, hg_ref, lb_ref, nw_ref, mat_ref, mask_ref,
               o_ref, st_ref):
    i = pl.program_id(1)

    @pl.when(i == 0)
    def _():
        st_ref[...] = jnp.zeros_like(st_ref)

    for r0 in range(0, hq_ref.shape[0], tile):
        _hgrn_tile(tile, levels, slice(r0, r0 + tile), hq_ref, hf_ref, hi_ref, hg_ref, lb_ref, nw_ref,
                   mat_ref, mask_ref, o_ref, st_ref)


def _hgrn_tile(tile, levels, rows, hq_ref, hf_ref, hi_ref, hg_ref, lb_ref, nw_ref, mat_ref, mask_ref,
               o_ref, st_ref):
    heads = [slice(hd * HGRN_DK, (hd + 1) * HGRN_DK) for hd in range(HGRN_HEADS)]
    lb = lb_ref[...]
    z = hf_ref[rows, :].astype(F32)
    la = jnp.log(lb)
    lc = jnp.log1p(-lb) + (jnp.minimum(z, 0.0) - jnp.log1p(jnp.exp(-jnp.abs(z))))
    logf = jnp.maximum(la, lc) + jnp.log1p(jnp.exp(-jnp.abs(la - lc)))
    kk = (1.0 - lb) / (1.0 + jnp.exp(z))
    xq = hq_ref[rows, :].astype(F32)
    q = xq * _sigmoid(xq)
    xg = hg_ref[rows, :].astype(F32)
    g = xg * _sigmoid(xg)
    v = hi_ref[rows, :]

    lf2 = logf * math.log2(math.e)
    lf_hi = lf2.astype(BF16)
    lf_lo = (lf2 - lf_hi.astype(F32)).astype(BF16)
    dall = jnp.dot(mat_ref[...], jnp.concatenate([lf_hi, lf_lo], axis=0),
                   preferred_element_type=F32)
    bcum = dall[0:tile]
    blast = bcum[tile - 1:tile, :]
    q_b = q.astype(BF16)
    k_b = kk.astype(BF16)

    a = [mask_ref[0] * lax.dot_general(q_b[:, c], k_b[:, c], _NT, preferred_element_type=F32)
         for c in heads]
    for j in range(levels):
        wj = jnp.exp2(dall[(j + 1) * tile:(j + 2) * tile])
        qw = (q * wj).astype(BF16)
        kw = (kk * wj).astype(BF16)
        mj = mask_ref[j + 1]
        a = [a[hd] + mj * lax.dot_general(qw[:, c], kw[:, c], _NT, preferred_element_type=F32)
             for hd, c in enumerate(heads)]
    q_dec = (q * jnp.exp2(bcum)).astype(BF16)
    k_dec = (kk * jnp.exp2(blast - bcum)).astype(BF16)
    s_dec = jnp.exp2(blast)
    for hd, c in enumerate(heads):
        st = st_ref[hd]
        o = jnp.dot(a[hd].astype(BF16), v[:, c], preferred_element_type=F32)
        o = o + lax.dot_general(q_dec[:, c], st.astype(BF16), _NT, preferred_element_type=F32)
        st_ref[hd] = st * s_dec[:, c] + lax.dot_general(
            v[:, c], k_dec[:, c], (((0,), (0,)), ((), ())), preferred_element_type=F32)
        o_ref[rows, c] = (_rms_rows(o, nw_ref[...]) * g[:, c]).astype(BF16)


def _hgrn(proj_main, lb, norm_w, l, b, s):
    tile = L_HGRN
    step = HGRN_STEP
    nt = s // step
    levels, mats, masks = _hgrn_tables(tile)
    width = HGRN_HEADS * HGRN_DK
    row = lambda bi, i: bi * nt + i
    return pl.pallas_call(
        functools.partial(_hgrn_body, tile, levels),
        grid=(b, nt),
        in_specs=[
            pl.BlockSpec((step, width), lambda bi, i: (row(bi, i), 4)),
            pl.BlockSpec((step, width), lambda bi, i: (row(bi, i), 5)),
            pl.BlockSpec((step, width), lambda bi, i: (row(bi, i), 6)),
            pl.BlockSpec((step, width), lambda bi, i: (row(bi, i), 7)),
            pl.BlockSpec((None, 1, width), lambda bi, i: (l, 0, 0)),
            pl.BlockSpec((None, 1, HGRN_DV), lambda bi, i: (l, 0, 0)),
            pl.BlockSpec(((levels + 1) * tile, 2 * tile), lambda bi, i: (0, 0)),
            pl.BlockSpec((levels + 1, tile, tile), lambda bi, i: (0, 0, 0)),
        ],
        out_specs=pl.BlockSpec((step, width), lambda bi, i: (row(bi, i), 0)),
        out_shape=jax.ShapeDtypeStruct((b * s, width), BF16),
        scratch_shapes=[pltpu.VMEM((HGRN_HEADS, HGRN_DV, HGRN_DK), F32)],
        compiler_params=_params(("parallel", "arbitrary"), 32),
        name="hgrn2",
    )(proj_main, proj_main, proj_main, proj_main, lb, norm_w,
      jnp.asarray(mats, BF16), jnp.asarray(masks, F32))


def _out_proj_body(ya_ref, yb_ref, yc_ref, w_ref, h_ref, nw_ref, o_ref, xn_ref):
    na = ya_ref.shape[1]
    nb = na + yb_ref.shape[1]
    for r in range(0, TM_OUT, TR_OUT):
        rows = slice(r, r + TR_OUT)
        acc = jnp.dot(ya_ref[rows, :], w_ref[0:na, :], preferred_element_type=F32)
        acc = acc + jnp.dot(yb_ref[rows, :], w_ref[na:nb, :], preferred_element_type=F32)
        acc = acc + jnp.dot(yc_ref[rows, :], w_ref[nb:, :], preferred_element_type=F32)
        hn = h_ref[rows, :] + acc
        o_ref[rows, :] = hn
        xn_ref[rows, :] = _rms_rows(hn, nw_ref[...]).astype(BF16)


def _out_proj(ya, yb, yc, w_out, h, ffn_norm_w, l):
    t, d = h.shape
    dm = w_out.shape[1]
    return pl.pallas_call(
        _out_proj_body,
        grid=(t // TM_OUT,),
        in_specs=[
            pl.BlockSpec((TM_OUT, ya.shape[1]), lambda i: (i, 0)),
            pl.BlockSpec((TM_OUT, yb.shape[1]), lambda i: (i, 0)),
            pl.BlockSpec((TM_OUT, yc.shape[1]), lambda i: (i, 0)),
            pl.BlockSpec((None, dm, d), lambda i: (l, 0, 0), pipeline_mode=pl.Buffered(1)),
            pl.BlockSpec((TM_OUT, d), lambda i: (i, 0)),
            pl.BlockSpec((None, 1, d), lambda i: (l, 0, 0)),
        ],
        out_specs=[pl.BlockSpec((TM_OUT, d), lambda i: (i, 0)), pl.BlockSpec((TM_OUT, d), lambda i: (i, 0))],
        out_shape=[jax.ShapeDtypeStruct((t, d), F32), jax.ShapeDtypeStruct((t, d), BF16)],
        compiler_params=_params(("parallel",), 56),
        name="out_proj",
    )(ya, yb, yc, w_out, h, ffn_norm_w)


def _ffn_up_body(xn_ref, wg_ref, wu_ref, o_ref):
    for r in range(0, TM_UP, TR_UP):
        xn = xn_ref[r:r + TR_UP, :]
        for c in range(0, TF_UP, 256):
            gate = jnp.dot(xn, wg_ref[:, c:c + 256], preferred_element_type=F32)
            up = jnp.dot(xn, wu_ref[:, c:c + 256], preferred_element_type=F32)
            o_ref[r:r + TR_UP, c:c + 256] = (gate * _sigmoid(gate) * up).astype(BF16)


def _ffn_up(xn, w_gate, w_up, l):
    t, d = xn.shape
    f = w_gate.shape[2]
    return pl.pallas_call(
        _ffn_up_body,
        grid=(t // TM_UP, f // TF_UP),
        in_specs=[
            pl.BlockSpec((TM_UP, d), lambda i, j: (i, 0)),
            pl.BlockSpec((None, d, TF_UP), lambda i, j: (l, 0, j)),
            pl.BlockSpec((None, d, TF_UP), lambda i, j: (l, 0, j)),
        ],
        out_specs=pl.BlockSpec((TM_UP, TF_UP), lambda i, j: (i, j)),
        out_shape=jax.ShapeDtypeStruct((t, f), BF16),
        compiler_params=_params(("parallel", "arbitrary"), 40),
        name="ffn_up",
    )(xn, w_gate, w_up)


def _ffn_down_body(last, a_ref, w_ref, h_ref, fw_ref, o_ref):
    d = o_ref.shape[1]
    for c in range(0, d, TN_DOWN):
        o_ref[:, c:c + TN_DOWN] = h_ref[:, c:c + TN_DOWN] + jnp.dot(
            a_ref[...], w_ref[:, c:c + TN_DOWN], preferred_element_type=F32)
    if last:
        o_ref[...] = _rms_rows(o_ref[...], fw_ref[...])


def _ffn_down(ff, w_down, h, final_w, l, last):
    t, d = h.shape
    f = ff.shape[1]
    return pl.pallas_call(
        functools.partial(_ffn_down_body, last),
        grid=(t // TM_DOWN,),
        in_specs=[
            pl.BlockSpec((TM_DOWN, f), lambda i: (i, 0)),
            pl.BlockSpec((None, f, d), lambda i: (l, 0, 0), pipeline_mode=pl.Buffered(1)),
            pl.BlockSpec((TM_DOWN, d), lambda i: (i, 0)),
            pl.BlockSpec((1, d), lambda i: (0, 0)),
        ],
        out_specs=pl.BlockSpec((TM_DOWN, d), lambda i: (i, 0)),
        out_shape=jax.ShapeDtypeStruct((t, d), F32),
        compiler_params=_params(("parallel",), 58),
        name="ffn_down",
    )(ff, w_down, h, final_w)


def _rope_swap(cols):
    half = QK_ROPE // 2
    return np.concatenate([cols[half:], cols[:half]])


def _w_in_columns():
    kr0 = Q_LORA + KV_LORA
    main = np.concatenate([np.arange(0, kr0), np.arange(kr0 + QK_ROPE, kr0 + QK_ROPE + MAIN_COLS - kr0)])
    kr = np.arange(kr0, kr0 + QK_ROPE)
    return main, np.concatenate([kr, _rope_swap(kr)])


def _w_uq_columns():
    per = QK_NOPE + QK_ROPE
    out = []
    for hd in range(MLA_HEADS):
        pe = np.arange(hd * per + QK_NOPE, (hd + 1) * per)
        out += [np.arange(hd * per, hd * per + QK_NOPE), pe, _rope_swap(pe)]
    return np.concatenate(out)


def _w_ukv_columns():
    per = QK_NOPE + V_HEAD
    kn = [np.arange(hd * per, hd * per + QK_NOPE) for hd in range(MLA_HEADS)]
    vv = [np.arange(hd * per + QK_NOPE, (hd + 1) * per) for hd in range(MLA_HEADS)]
    return np.concatenate(kn), np.concatenate(vv)


@jax.jit
def _trunk(x, positions, attn_norm_w, w_in, q_norm_w, w_uq, kv_norm_w, w_ukv, mla_out_norm_w,
           conv_w, conv_b, conv_ln_w, conv_ln_b, hgrn_lower_bounds, hgrn_norm_w, w_out,
           ffn_norm_w, w_gate, w_up, w_down, final_norm_w):
    b, s, d = x.shape
    depth = w_in.shape[0]
    t = b * s
    for rows in (TM_IN, TM_OUT, TM_UP, TM_DOWN):
        assert t % rows == 0, (t, rows)
    for rows in (2 * TS_QKV, TS_CONV, HGRN_STEP):
        assert s % rows == 0, (s, rows)

    inv_freq = ROPE_THETA ** (-jnp.arange(0, QK_ROPE, 2, dtype=F32) / QK_ROPE)
    ang = positions.astype(F32)[..., None] * inv_freq
    cos, sin = jnp.cos(ang), jnp.sin(ang)
    cs = jnp.concatenate([cos, cos, -sin, sin], axis=-1).reshape(t, 2 * QK_ROPE)
    cs_t = cs.T
    lb_all = jnp.cumsum(jax.nn.softmax(hgrn_lower_bounds.astype(F32), axis=0), axis=0)
    lb_all = (lb_all - lb_all[0:1])[:, None, :]

    main_cols, kr_cols = _w_in_columns()
    kr0 = int(kr_cols[0])
    assert np.array_equal(main_cols, np.r_[0:kr0, kr0 + QK_ROPE:w_in.shape[2]])
    w_main = jnp.concatenate([w_in[:, :, :kr0], w_in[:, :, kr0 + QK_ROPE:]], axis=2).astype(BF16)
    w_kr = w_in[:, :, kr_cols].astype(BF16)
    w_uq_t = jnp.swapaxes(w_uq[:, :, _w_uq_columns()], 1, 2).astype(BF16)
    kn_cols, v_cols = _w_ukv_columns()
    w_k_b = w_ukv[:, :, kn_cols].astype(BF16)
    w_v_t = jnp.swapaxes(w_ukv[:, :, v_cols], 1, 2).astype(BF16)
    w_out_b = w_out.astype(BF16)
    w_gate_b = w_gate.astype(BF16)
    w_up_b = w_up.astype(BF16)
    w_down_b = w_down.astype(BF16)
    row3 = lambda a: a[:, None, :]

    h = x.reshape(t, d)
    for l in range(depth):
        proj_main, kr = _in_proj(h, row3(attn_norm_w), w_main, w_kr, l)
        q_t = _q_proj(proj_main, row3(q_norm_w), w_uq_t, cs_t, l, b, s)
        kcat, v_t = _kv_proj(proj_main, kr, row3(kv_norm_w), w_k_b, w_v_t, cs, l, b, s)
        ya = _attention(q_t, kcat, v_t, row3(mla_out_norm_w), l, b, s)
        yb = _conv(proj_main, conv_w, row3(conv_b), row3(conv_ln_w), row3(conv_ln_b), l, b, s)
        yc = _hgrn(proj_main, lb_all, row3(hgrn_norm_w), l, b, s)
        h, xn = _out_proj(ya, yb, yc, w_out_b, h, row3(ffn_norm_w), l)
        ff = _ffn_up(xn, w_gate_b, w_up_b, l)
        h = _ffn_down(ff, w_down_b, h, final_norm_w[None, :], l, last=(l == depth - 1))
    return h.reshape(b, s, d)


def kernel(x, positions, attn_norm_w, w_in, q_norm_w, w_uq, kv_norm_w, w_ukv, mla_out_norm_w,
           conv_w, conv_b, conv_ln_w, conv_ln_b, hgrn_lower_bounds, hgrn_norm_w, w_out,
           ffn_norm_w, w_gate, w_up, w_down, final_norm_w):
    return _trunk(x, positions, attn_norm_w, w_in, q_norm_w, w_uq, kv_norm_w, w_ukv, mla_out_norm_w,
                  conv_w, conv_b, conv_ln_w, conv_ln_b, hgrn_lower_bounds, hgrn_norm_w, w_out,
                  ffn_norm_w, w_gate, w_up, w_down, final_norm_w)
```

```python
import functools
import math

import jax
import jax.numpy as jnp
import numpy as np
from jax import lax
from jax.experimental import pallas as pl
from jax.experimental.pallas import tpu as pltpu

F32 = jnp.float32
BF16 = jnp.bfloat16

MLA_HEADS = 8
QK_NOPE = 128
QK_ROPE = 64
V_HEAD = 128
Q_LORA = 512
KV_LORA = 512
ROPE_THETA = 10000.0
CONV_CH = 512
CONV_WIDTH = 31
HGRN_HEADS = 4
HGRN_DK = 128
HGRN_DV = 128
RMS_EPS = 1e-6
LN_EPS = 1e-5
QK_CAT = 256
V_ROWS = V_HEAD + 16
MAIN_COLS = 4096
CONV_HALO = 32

TM_IN = 1024
TR_IN = 512
TN_IN = 512
TS_QKV = 512
TS_CONV = 512
L_HGRN = 128
HGRN_STEP = 512
TM_OUT = 512
TR_OUT = 256
TM_UP = 2048
TR_UP = 512
TF_UP = 512
TM_DOWN = 512
TN_DOWN = 512
MIB = 1024 * 1024


def _params(sem, vmem_mib):
    return pltpu.CompilerParams(dimension_semantics=sem, vmem_limit_bytes=vmem_mib * MIB)


def _rms_rows(x, w):
    ms = jnp.mean(x * x, axis=-1, keepdims=True)
    return x * lax.rsqrt(ms + RMS_EPS) * w


def _sigmoid(x):
    return 1.0 / (1.0 + jnp.exp(-x))


def _in_proj_body(x_ref, nw_ref, wm_ref, wk_ref, om_ref, ok_ref):
    for r in range(0, TM_IN, TR_IN):
        xn = _rms_rows(x_ref[r:r + TR_IN, :], nw_ref[...]).astype(BF16)
        for c in range(0, MAIN_COLS, TN_IN):
            om_ref[r:r + TR_IN, c:c + TN_IN] = jnp.dot(
                xn, wm_ref[:, c:c + TN_IN], preferred_element_type=F32).astype(BF16)
        ok_ref[r:r + TR_IN, :] = jnp.dot(xn, wk_ref[...], preferred_element_type=F32)


def _in_proj(h, norm_w, w_main, w_kr, l):
    t, d = h.shape
    once = pl.Buffered(1)
    return pl.pallas_call(
        _in_proj_body,
        grid=(t // TM_IN,),
        in_specs=[
            pl.BlockSpec((TM_IN, d), lambda i: (i, 0)),
            pl.BlockSpec((None, 1, d), lambda i: (l, 0, 0)),
            pl.BlockSpec((None, d, MAIN_COLS), lambda i: (l, 0, 0), pipeline_mode=once),
            pl.BlockSpec((None, d, 128), lambda i: (l, 0, 0), pipeline_mode=once),
        ],
        out_specs=[
            pl.BlockSpec((TM_IN, MAIN_COLS), lambda i: (i, 0)),
            pl.BlockSpec((TM_IN, 128), lambda i: (i, 0)),
        ],
        out_shape=[jax.ShapeDtypeStruct((t, MAIN_COLS), BF16), jax.ShapeDtypeStruct((t, 128), F32)],
        compiler_params=_params(("parallel",), 58),
        name="in_proj",
    )(h, norm_w, w_main, w_kr)


_NT = (((1,), (1,)), ((), ()))


def _q_proj_body(scale, c_ref, nw_ref, w_ref, cs_ref, o_ref):
    cn = _rms_rows(c_ref[...].astype(F32), nw_ref[...]).astype(BF16)
    cs = cs_ref[...] * scale
    for hd in range(MLA_HEADS):
        qt = lax.dot_general(w_ref[hd * QK_CAT:(hd + 1) * QK_CAT, :], cn, _NT,
                             preferred_element_type=F32)
        o_ref[hd, 0:QK_NOPE, :] = (qt[0:QK_NOPE, :] * scale).astype(BF16)
        o_ref[hd, QK_NOPE:QK_CAT, :] = (qt[QK_NOPE:QK_CAT, :] * cs).astype(BF16)


def _q_proj(proj_main, norm_w, w_uq_t, cs_t, l, b, s):
    tq = 2 * TS_QKV
    ns = s // tq
    scale = (QK_NOPE + QK_ROPE) ** -0.5 * math.log2(math.e)
    return pl.pallas_call(
        functools.partial(_q_proj_body, scale),
        grid=(b, ns),
        in_specs=[
            pl.BlockSpec((tq, Q_LORA), lambda bi, i: (bi * ns + i, 0)),
            pl.BlockSpec((None, 1, Q_LORA), lambda bi, i: (l, 0, 0)),
            pl.BlockSpec((None, MLA_HEADS * QK_CAT, Q_LORA), lambda bi, i: (l, 0, 0)),
            pl.BlockSpec((128, tq), lambda bi, i: (0, bi * ns + i)),
        ],
        out_specs=pl.BlockSpec((None, MLA_HEADS, None, QK_CAT, tq), lambda bi, i: (bi, 0, i, 0, 0)),
        out_shape=jax.ShapeDtypeStruct((b, MLA_HEADS, ns, QK_CAT, tq), BF16),
        compiler_params=_params(("parallel", "parallel"), 40),
        name="q_proj",
    )(proj_main, norm_w, w_uq_t, cs_t)


def _kv_proj_body(c_ref, kr_ref, nw_ref, wk_ref, wv_ref, cs_ref, k_ref, v_ref):
    cn = _rms_rows(c_ref[...].astype(F32), nw_ref[...]).astype(BF16)
    a = kr_ref[...] * cs_ref[...]
    krot = (a + pltpu.roll(a, 64, axis=1)).astype(BF16)
    ts = cn.shape[0]
    for h0 in range(0, MLA_HEADS, 2):
        kn = jnp.dot(cn, wk_ref[:, h0 * QK_NOPE:(h0 + 2) * QK_NOPE],
                     preferred_element_type=F32)
        vt = lax.dot_general(wv_ref[h0 * V_HEAD:(h0 + 2) * V_HEAD, :], cn, _NT,
                             preferred_element_type=F32)
        for d in range(2):
            hd = h0 + d
            k_ref[hd, :, 0:QK_NOPE] = kn[:, d * QK_NOPE:(d + 1) * QK_NOPE].astype(BF16)
            k_ref[hd, :, QK_NOPE:QK_CAT] = krot
            v_ref[hd, 0:V_HEAD, :] = vt[d * V_HEAD:(d + 1) * V_HEAD, :].astype(BF16)
            v_ref[hd, V_HEAD:V_ROWS, :] = jnp.ones((V_ROWS - V_HEAD, ts), BF16)


def _kv_proj(proj_main, kr, norm_w, w_k, w_v_t, cs, l, b, s):
    ns = s // TS_QKV
    return pl.pallas_call(
        _kv_proj_body,
        grid=(b, ns),
        in_specs=[
            pl.BlockSpec((TS_QKV, KV_LORA), lambda bi, i: (bi * ns + i, 1)),
            pl.BlockSpec((TS_QKV, 128), lambda bi, i: (bi * ns + i, 0)),
            pl.BlockSpec((None, 1, KV_LORA), lambda bi, i: (l, 0, 0)),
            pl.BlockSpec((None, KV_LORA, MLA_HEADS * QK_NOPE), lambda bi, i: (l, 0, 0)),
            pl.BlockSpec((None, MLA_HEADS * V_HEAD, KV_LORA), lambda bi, i: (l, 0, 0)),
            pl.BlockSpec((TS_QKV, 128), lambda bi, i: (bi * ns + i, 0)),
        ],
        out_specs=[
            pl.BlockSpec((None, MLA_HEADS, TS_QKV, QK_CAT), lambda bi, i: (bi, 0, i, 0)),
            pl.BlockSpec((None, MLA_HEADS, None, V_ROWS, TS_QKV), lambda bi, i: (bi, 0, i, 0, 0)),
        ],
        out_shape=[
            jax.ShapeDtypeStruct((b, MLA_HEADS, s, QK_CAT), BF16),
            jax.ShapeDtypeStruct((b, MLA_HEADS, ns, V_ROWS, TS_QKV), BF16),
        ],
        compiler_params=_params(("parallel", "parallel"), 40),
        name="kv_proj",
    )(proj_main, kr, norm_w, w_k, w_v_t, cs)


def _attn_body(tk, nq, q_ref, k_ref, v_ref, nw_ref, o_ref,
               sa_ref, sb_ref, xa_ref, xb_ref, m_ref, acc_ref):
    tq = 2 * tk
    key = lax.broadcasted_iota(jnp.int32, (tk, tq), 0)
    qry = lax.broadcasted_iota(jnp.int32, (tk, tq), 1)
    causal = key <= qry
    causal_half = causal[:, 0:tk]

    for qi in range(nq):
        q0 = qi * tq

        def fill(s_ref, x_ref, j, lo=0, hi=tq, mask=None):
            start = j * tk if isinstance(j, int) else pl.multiple_of(j * tk, tk)
            sc = jnp.dot(k_ref[pl.ds(start, tk), :], q_ref[qi, :, lo:hi],
                         preferred_element_type=F32)
            if mask is not None:
                sc = jnp.where(mask, sc, -jnp.inf)
            s_ref[:, 0:hi - lo] = sc
            x_ref[:, 0:hi - lo] = jnp.max(sc, axis=0, keepdims=True)

        def update(s_ref, x_ref, j, lo=0, hi=tq):
            m = m_ref[:, lo:hi]
            m_new = jnp.maximum(m, x_ref[:, 0:hi - lo])
            p = jnp.exp2(s_ref[:, 0:hi - lo] - m_new)
            alpha = jnp.exp2(m - m_new)
            acc_ref[:, lo:hi] = alpha * acc_ref[:, lo:hi] + jnp.dot(
                v_ref[j], p.astype(BF16), preferred_element_type=F32)
            m_ref[:, lo:hi] = m_new

        m_ref[...] = jnp.full(m_ref.shape, -jnp.inf, F32)
        acc_ref[...] = jnp.zeros(acc_ref.shape, F32)
        first_mask = causal if qi == 0 else None
        fill(sa_ref, xa_ref, 0, mask=first_mask)

        def pair(jj):
            j0 = 2 * jj
            fill(sb_ref, xb_ref, j0 + 1)
            update(sa_ref, xa_ref, j0)
            fill(sa_ref, xa_ref, j0 + 2)
            update(sb_ref, xb_ref, j0 + 1)

        def two_pairs(it, carry):
            pair(2 * it)
            pair(2 * it + 1)
            return carry

        full = max(qi - 1, 0)
        if full >= 2:
            lax.fori_loop(0, full // 2, two_pairs, 0)
        if full % 2:
            pair(full - 1)
        if qi > 0:
            j0 = 2 * (qi - 1)
            fill(sb_ref, xb_ref, j0 + 1)
            update(sa_ref, xa_ref, j0)
            fill(sa_ref, xa_ref, j0 + 2, mask=causal)
            update(sb_ref, xb_ref, j0 + 1)
        fill(sb_ref, xb_ref, 2 * qi + 1, tk, tq, mask=causal_half)
        update(sa_ref, xa_ref, 2 * qi)
        update(sb_ref, xb_ref, 2 * qi + 1, tk, tq)
        o = (acc_ref[0:V_HEAD, :] / acc_ref[V_HEAD:V_HEAD + 1, :]).T
        o_ref[q0:q0 + tq, :] = _rms_rows(o, nw_ref[...]).astype(BF16)


def _attention(q_t, kcat, v_t, norm_w, l, b, s):
    tk = TS_QKV
    tq = 2 * tk
    nq = s // tq
    pitch = tq + 128
    return pl.pallas_call(
        functools.partial(_attn_body, tk, nq),
        grid=(b, MLA_HEADS),
        in_specs=[
            pl.BlockSpec((None, None, nq, QK_CAT, tq), lambda bi, hi: (bi, hi, 0, 0, 0)),
            pl.BlockSpec((None, None, s, QK_CAT), lambda bi, hi: (bi, hi, 0, 0)),
            pl.BlockSpec((None, None, s // tk, V_ROWS, tk), lambda bi, hi: (bi, hi, 0, 0, 0)),
            pl.BlockSpec((None, 1, V_HEAD), lambda bi, hi: (l, 0, 0)),
        ],
        out_specs=pl.BlockSpec((s, V_HEAD), lambda bi, hi: (bi, hi)),
        out_shape=jax.ShapeDtypeStruct((b * s, MLA_HEADS * V_HEAD), BF16),
        scratch_shapes=[
            pltpu.VMEM((tk, pitch), F32), pltpu.VMEM((tk, pitch), F32),
            pltpu.VMEM((1, tq), F32), pltpu.VMEM((1, tq), F32),
            pltpu.VMEM((1, tq), F32), pltpu.VMEM((V_ROWS, tq), F32),
        ],
        compiler_params=_params(("parallel", "parallel"), 48),
        name="mla_attention",
    )(q_t, kcat, v_t, norm_w)


def _conv_body(ts, u_ref, halo_ref, cw_ref, cb_ref, lw_ref, lb_ref, o_ref, hp_ref):
    i = pl.program_id(1)

    def glu(u):
        u = u.astype(F32)
        return u[:, 0:CONV_CH] * _sigmoid(u[:, CONV_CH:2 * CONV_CH])

    hp_ref[CONV_HALO:CONV_HALO + ts, :] = glu(u_ref[...])
    hp_ref[0:CONV_HALO, :] = jnp.where(i == 0, 0.0, glu(halo_ref[...]))
    off = CONV_HALO - (CONV_WIDTH - 1)
    acc = jnp.zeros((ts, CONV_CH), F32)
    hp = hp_ref[...]
    rows = hp.shape[0]
    for r in range(8):
        last = (CONV_WIDTH - 1 - r) // 8
        xr = pltpu.roll(hp, rows - (off + r), axis=0)
        for g in range(last + 1):
            acc = acc + xr[8 * g:8 * g + ts, :] * cw_ref[8 * g + r:8 * g + r + 1, :]
    acc = acc + cb_ref[...]
    mu = jnp.mean(acc, axis=-1, keepdims=True)
    xc = acc - mu
    var = jnp.mean(xc * xc, axis=-1, keepdims=True)
    y = xc * lax.rsqrt(var + LN_EPS) * lw_ref[...] + lb_ref[...]
    o_ref[...] = (y * _sigmoid(y)).astype(BF16)


def _conv(proj_main, conv_w, conv_b, ln_w, ln_b, l, b, s):
    ns = s // TS_CONV
    per = TS_CONV // CONV_HALO

    def halo_map(bi, i):
        return (jnp.maximum((bi * ns + i) * per - 1, 0), 1)

    return pl.pallas_call(
        functools.partial(_conv_body, TS_CONV),
        grid=(b, ns),
        in_specs=[
            pl.BlockSpec((TS_CONV, 2 * CONV_CH), lambda bi, i: (bi * ns + i, 1)),
            pl.BlockSpec((CONV_HALO, 2 * CONV_CH), halo_map),
            pl.BlockSpec((None, CONV_WIDTH, CONV_CH), lambda bi, i: (l, 0, 0)),
            pl.BlockSpec((None, 1, CONV_CH), lambda bi, i: (l, 0, 0)),
            pl.BlockSpec((None, 1, CONV_CH), lambda bi, i: (l, 0, 0)),
            pl.BlockSpec((None, 1, CONV_CH), lambda bi, i: (l, 0, 0)),
        ],
        out_specs=pl.BlockSpec((TS_CONV, CONV_CH), lambda bi, i: (bi * ns + i, 0)),
        out_shape=jax.ShapeDtypeStruct((b * s, CONV_CH), BF16),
        scratch_shapes=[pltpu.VMEM((CONV_HALO + TS_CONV, CONV_CH), F32)],
        compiler_params=_params(("parallel", "parallel"), 32),
        name="conv_module",
    )(proj_main, proj_main, conv_w, conv_b, ln_w, ln_b)


def _hgrn_tables(tile):
    levels = int(math.log2(tile))
    t = np.arange(tile)
    tri = (t[None, :] <= t[:, None]).astype(np.float32)
    mats = [tri]
    masks = [np.eye(tile, dtype=np.float32)]
    for j in range(levels):
        n = 1 << j
        t0 = (t & ~(2 * n - 1)) + n - 1
        diff = tri - (t[None, :] <= t0[:, None]).astype(np.float32)
        is_lower = (((t >> j) & 1) == 1)[:, None]
        mats.append(np.where(is_lower, diff, -diff))
        same_block = (t[:, None] >> (j + 1)) == (t[None, :] >> (j + 1))
        lower = ((t[:, None] >> j) & 1) == 1
        upper = ((t[None, :] >> j) & 1) == 0
        masks.append((same_block & lower & upper).astype(np.float32))
    stacked = np.concatenate(mats, axis=0)
    return levels, np.concatenate([stacked, stacked], axis=1), np.stack(masks, axis=0)


def _hgrn_body(tile, levels, hq_ref, hf_ref, hi_ref, hg_ref, lb_ref, nw_ref, mat_ref, mask_ref,
               o_ref, st_ref):
    i = pl.program_id(1)

    @pl.when(i == 0)
    def _():
        st_ref[...] = jnp.zeros_like(st_ref)

    for r0 in range(0, hq_ref.shape[0], tile):
        _hgrn_tile(tile, levels, slice(r0, r0 + tile), hq_ref, hf_ref, hi_ref, hg_ref, lb_ref, nw_ref,
                   mat_ref, mask_ref, o_ref, st_ref)


def _hgrn_tile(tile, levels, rows, hq_ref, hf_ref, hi_ref, hg_ref, lb_ref, nw_ref, mat_ref, mask_ref,
               o_ref, st_ref):
    heads = [slice(hd * HGRN_DK, (hd + 1) * HGRN_DK) for hd in range(HGRN_HEADS)]
    lb = lb_ref[...]
    z = hf_ref[rows, :].astype(F32)
    la = jnp.log(lb)
    lc = jnp.log1p(-lb) + (jnp.minimum(z, 0.0) - jnp.log1p(jnp.exp(-jnp.abs(z))))
    logf = jnp.maximum(la, lc) + jnp.log1p(jnp.exp(-jnp.abs(la - lc)))
    kk = (1.0 - lb) / (1.0 + jnp.exp(z))
    xq = hq_ref[rows, :].astype(F32)
    q = xq * _sigmoid(xq)
    xg = hg_ref[rows, :].astype(F32)
    g = xg * _sigmoid(xg)
    v = hi_ref[rows, :]

    lf2 = logf * math.log2(math.e)
    lf_hi = lf2.astype(BF16)
    lf_lo = (lf2 - lf_hi.astype(F32)).astype(BF16)
    dall = jnp.dot(mat_ref[...], jnp.concatenate([lf_hi, lf_lo], axis=0),
                   preferred_element_type=F32)
    bcum = dall[0:tile]
    blast = bcum[tile - 1:tile, :]
    q_b = q.astype(BF16)
    k_b = kk.astype(BF16)

    a = [mask_ref[0] * lax.dot_general(q_b[:, c], k_b[:, c], _NT, preferred_element_type=F32)
         for c in heads]
    for j in range(levels):
        wj = jnp.exp2(dall[(j + 1) * tile:(j + 2) * tile])
        qw = (q * wj).astype(BF16)
        kw = (kk * wj).astype(BF16)
        mj = mask_ref[j + 1]
        a = [a[hd] + mj * lax.dot_general(qw[:, c], kw[:, c], _NT, preferred_element_type=F32)
             for hd, c in enumerate(heads)]
    q_dec = (q * jnp.exp2(bcum)).astype(BF16)
    k_dec = (kk * jnp.exp2(blast - bcum)).astype(BF16)
    s_dec = jnp.exp2(blast)
    for hd, c in enumerate(heads):
        st = st_ref[hd]
        o = jnp.dot(a[hd].astype(BF16), v[:, c], preferred_element_type=F32)
        o = o + lax.dot_general(q_dec[:, c], st.astype(BF16), _NT, preferred_element_type=F32)
        st_ref[hd] = st * s_dec[:, c] + lax.dot_general(
            v[:, c], k_dec[:, c], (((0,), (0,)), ((), ())), preferred_element_type=F32)
        o_ref[rows, c] = (_rms_rows(o, nw_ref[...]) * g[:, c]).astype(BF16)


def _hgrn(proj_main, lb, norm_w, l, b, s):
    tile = L_HGRN
    step = HGRN_STEP
    nt = s // step
    levels, mats, masks = _hgrn_tables(tile)
    width = HGRN_HEADS * HGRN_DK
    row = lambda bi, i: bi * nt + i
    return pl.pallas_call(
        functools.partial(_hgrn_body, tile, levels),
        grid=(b, nt),
        in_specs=[
            pl.BlockSpec((step, width), lambda bi, i: (row(bi, i), 4)),
            pl.BlockSpec((step, width), lambda bi, i: (row(bi, i), 5)),
            pl.BlockSpec((step, width), lambda bi, i: (row(bi, i), 6)),
            pl.BlockSpec((step, width), lambda bi, i: (row(bi, i), 7)),
            pl.BlockSpec((None, 1, width), lambda bi, i: (l, 0, 0)),
            pl.BlockSpec((None, 1, HGRN_DV), lambda bi, i: (l, 0, 0)),
            pl.BlockSpec(((levels + 1) * tile, 2 * tile), lambda bi, i: (0, 0)),
            pl.BlockSpec((levels + 1, tile, tile), lambda bi, i: (0, 0, 0)),
        ],
        out_specs=pl.BlockSpec((step, width), lambda bi, i: (row(bi, i), 0)),
        out_shape=jax.ShapeDtypeStruct((b * s, width), BF16),
        scratch_shapes=[pltpu.VMEM((HGRN_HEADS, HGRN_DV, HGRN_DK), F32)],
        compiler_params=_params(("parallel", "arbitrary"), 32),
        name="hgrn2",
    )(proj_main, proj_main, proj_main, proj_main, lb, norm_w,
      jnp.asarray(mats, BF16), jnp.asarray(masks, F32))


def _out_proj_body(ya_ref, yb_ref, yc_ref, w_ref, h_ref, nw_ref, o_ref, xn_ref):
    na = ya_ref.shape[1]
    nb = na + yb_ref.shape[1]
    for r in range(0, TM_OUT, TR_OUT):
        rows = slice(r, r + TR_OUT)
        acc = jnp.dot(ya_ref[rows, :], w_ref[0:na, :], preferred_element_type=F32)
        acc = acc + jnp.dot(yb_ref[rows, :], w_ref[na:nb, :], preferred_element_type=F32)
        acc = acc + jnp.dot(yc_ref[rows, :], w_ref[nb:, :], preferred_element_type=F32)
        hn = h_ref[rows, :] + acc
        o_ref[rows, :] = hn
        xn_ref[rows, :] = _rms_rows(hn, nw_ref[...]).astype(BF16)


def _out_proj(ya, yb, yc, w_out, h, ffn_norm_w, l):
    t, d = h.shape
    dm = w_out.shape[1]
    return pl.pallas_call(
        _out_proj_body,
        grid=(t // TM_OUT,),
        in_specs=[
            pl.BlockSpec((TM_OUT, ya.shape[1]), lambda i: (i, 0)),
            pl.BlockSpec((TM_OUT, yb.shape[1]), lambda i: (i, 0)),
            pl.BlockSpec((TM_OUT, yc.shape[1]), lambda i: (i, 0)),
            pl.BlockSpec((None, dm, d), lambda i: (l, 0, 0), pipeline_mode=pl.Buffered(1)),
            pl.BlockSpec((TM_OUT, d), lambda i: (i, 0)),
            pl.BlockSpec((None, 1, d), lambda i: (l, 0, 0)),
        ],
        out_specs=[pl.BlockSpec((TM_OUT, d), lambda i: (i, 0)), pl.BlockSpec((TM_OUT, d), lambda i: (i, 0))],
        out_shape=[jax.ShapeDtypeStruct((t, d), F32), jax.ShapeDtypeStruct((t, d), BF16)],
        compiler_params=_params(("parallel",), 56),
        name="out_proj",
    )(ya, yb, yc, w_out, h, ffn_norm_w)


def _ffn_up_body(xn_ref, wg_ref, wu_ref, o_ref):
    for r in range(0, TM_UP, TR_UP):
        xn = xn_ref[r:r + TR_UP, :]
        for c in range(0, TF_UP, 256):
            gate = jnp.dot(xn, wg_ref[:, c:c + 256], preferred_element_type=F32)
            up = jnp.dot(xn, wu_ref[:, c:c + 256], preferred_element_type=F32)
            o_ref[r:r + TR_UP, c:c + 256] = (gate * _sigmoid(gate) * up).astype(BF16)


def _ffn_up(xn, w_gate, w_up, l):
    t, d = xn.shape
    f = w_gate.shape[2]
    return pl.pallas_call(
        _ffn_up_body,
        grid=(t // TM_UP, f // TF_UP),
        in_specs=[
            pl.BlockSpec((TM_UP, d), lambda i, j: (i, 0)),
            pl.BlockSpec((None, d, TF_UP), lambda i, j: (l, 0, j)),
            pl.BlockSpec((None, d, TF_UP), lambda i, j: (l, 0, j)),
        ],
        out_specs=pl.BlockSpec((TM_UP, TF_UP), lambda i, j: (i, j)),
        out_shape=jax.ShapeDtypeStruct((t, f), BF16),
        compiler_params=_params(("parallel", "arbitrary"), 40),
        name="ffn_up",
    )(xn, w_gate, w_up)


def _ffn_down_body(last, a_ref, w_ref, h_ref, fw_ref, o_ref):
    d = o_ref.shape[1]
    for c in range(0, d, TN_DOWN):
        o_ref[:, c:c + TN_DOWN] = h_ref[:, c:c + TN_DOWN] + jnp.dot(
            a_ref[...], w_ref[:, c:c + TN_DOWN], preferred_element_type=F32)
    if last:
        o_ref[...] = _rms_rows(o_ref[...], fw_ref[...])


def _ffn_down(ff, w_down, h, final_w, l, last):
    t, d = h.shape
    f = ff.shape[1]
    return pl.pallas_call(
        functools.partial(_ffn_down_body, last),
        grid=(t // TM_DOWN,),
        in_specs=[
            pl.BlockSpec((TM_DOWN, f), lambda i: (i, 0)),
            pl.BlockSpec((None, f, d), lambda i: (l, 0, 0), pipeline_mode=pl.Buffered(1)),
            pl.BlockSpec((TM_DOWN, d), lambda i: (i, 0)),
            pl.BlockSpec((1, d), lambda i: (0, 0)),
        ],
        out_specs=pl.BlockSpec((TM_DOWN, d), lambda i: (i, 0)),
        out_shape=jax.ShapeDtypeStruct((t, d), F32),
        compiler_params=_params(("parallel",), 58),
        name="ffn_down",
    )(ff, w_down, h, final_w)


def _rope_swap(cols):
    half = QK_ROPE // 2
    return np.concatenate([cols[half:], cols[:half]])


def _w_in_columns():
    kr0 = Q_LORA + KV_LORA
    main = np.concatenate([np.arange(0, kr0), np.arange(kr0 + QK_ROPE, kr0 + QK_ROPE + MAIN_COLS - kr0)])
    kr = np.arange(kr0, kr0 + QK_ROPE)
    return main, np.concatenate([kr, _rope_swap(kr)])


def _w_uq_columns():
    per = QK_NOPE + QK_ROPE
    out = []
    for hd in range(MLA_HEADS):
        pe = np.arange(hd * per + QK_NOPE, (hd + 1) * per)
        out += [np.arange(hd * per, hd * per + QK_NOPE), pe, _rope_swap(pe)]
    return np.concatenate(out)


def _w_ukv_columns():
    per = QK_NOPE + V_HEAD
    kn = [np.arange(hd * per, hd * per + QK_NOPE) for hd in range(MLA_HEADS)]
    vv = [np.arange(hd * per + QK_NOPE, (hd + 1) * per) for hd in range(MLA_HEADS)]
    return np.concatenate(kn), np.concatenate(vv)


@jax.jit
def _trunk(x, positions, attn_norm_w, w_in, q_norm_w, w_uq, kv_norm_w, w_ukv, mla_out_norm_w,
           conv_w, conv_b, conv_ln_w, conv_ln_b, hgrn_lower_bounds, hgrn_norm_w, w_out,
           ffn_norm_w, w_gate, w_up, w_down, final_norm_w):
    b, s, d = x.shape
    depth = w_in.shape[0]
    t = b * s
    for rows in (TM_IN, TM_OUT, TM_UP, TM_DOWN):
        assert t % rows == 0, (t, rows)
    for rows in (2 * TS_QKV, TS_CONV, HGRN_STEP):
        assert s % rows == 0, (s, rows)

    inv_freq = ROPE_THETA ** (-jnp.arange(0, QK_ROPE, 2, dtype=F32) / QK_ROPE)
    ang = positions.astype(F32)[..., None] * inv_freq
    cos, sin = jnp.cos(ang), jnp.sin(ang)
    cs = jnp.concatenate([cos, cos, -sin, sin], axis=-1).reshape(t, 2 * QK_ROPE)
    cs_t = cs.T
    lb_all = jnp.cumsum(jax.nn.softmax(hgrn_lower_bounds.astype(F32), axis=0), axis=0)
    lb_all = (lb_all - lb_all[0:1])[:, None, :]

    main_cols, kr_cols = _w_in_columns()
    kr0 = int(kr_cols[0])
    assert np.array_equal(main_cols, np.r_[0:kr0, kr0 + QK_ROPE:w_in.shape[2]])
    w_in_b = w_in.astype(BF16)
    w_main = jnp.concatenate([w_in_b[:, :, :kr0], w_in_b[:, :, kr0 + QK_ROPE:]], axis=2)
    w_kr = w_in_b[:, :, kr_cols]
    w_uq_t = jnp.swapaxes(w_uq[:, :, _w_uq_columns()], 1, 2).astype(BF16)
    kn_cols, v_cols = _w_ukv_columns()
    w_k_b = w_ukv[:, :, kn_cols].astype(BF16)
    w_v_t = jnp.swapaxes(w_ukv[:, :, v_cols], 1, 2).astype(BF16)
    w_out_b = w_out.astype(BF16)
    w_gate_b = w_gate.astype(BF16)
    w_up_b = w_up.astype(BF16)
    w_down_b = w_down.astype(BF16)
    row3 = lambda a: a[:, None, :]

    h = x.reshape(t, d)
    for l in range(depth):
        proj_main, kr = _in_proj(h, row3(attn_norm_w), w_main, w_kr, l)
        q_t = _q_proj(proj_main, row3(q_norm_w), w_uq_t, cs_t, l, b, s)
        kcat, v_t = _kv_proj(proj_main, kr, row3(kv_norm_w), w_k_b, w_v_t, cs, l, b, s)
        ya = _attention(q_t, kcat, v_t, row3(mla_out_norm_w), l, b, s)
        yb = _conv(proj_main, conv_w, row3(conv_b), row3(conv_ln_w), row3(conv_ln_b), l, b, s)
        yc = _hgrn(proj_main, lb_all, row3(hgrn_norm_w), l, b, s)
        h, xn = _out_proj(ya, yb, yc, w_out_b, h, row3(ffn_norm_w), l)
        ff = _ffn_up(xn, w_gate_b, w_up_b, l)
        h = _ffn_down(ff, w_down_b, h, final_norm_w[None, :], l, last=(l == depth - 1))
    return h.reshape(b, s, d)


def kernel(x, positions, attn_norm_w, w_in, q_norm_w, w_uq, kv_norm_w, w_ukv, mla_out_norm_w,
           conv_w, conv_b, conv_ln_w, conv_ln_b, hgrn_lower_bounds, hgrn_norm_w, w_out,
           ffn_norm_w, w_gate, w_up, w_down, final_norm_w):
    return _trunk(x, positions, attn_norm_w, w_in, q_norm_w, w_uq, kv_norm_w, w_ukv, mla_out_norm_w,
                  conv_w, conv_b, conv_ln_w, conv_ln_b, hgrn_lower_bounds, hgrn_norm_w, w_out,
                  ffn_norm_w, w_gate, w_up, w_down, final_norm_w)
```

```python
import functools
import math

import jax
import jax.numpy as jnp
import numpy as np
from jax import lax
from jax.experimental import pallas as pl
from jax.experimental.pallas import tpu as pltpu

F32 = jnp.float32
BF16 = jnp.bfloat16

MLA_HEADS = 8
QK_NOPE = 128
QK_ROPE = 64
V_HEAD = 128
Q_LORA = 512
KV_LORA = 512
ROPE_THETA = 10000.0
CONV_CH = 512
CONV_WIDTH = 31
HGRN_HEADS = 4
HGRN_DK = 128
HGRN_DV = 128
RMS_EPS = 1e-6
LN_EPS = 1e-5
QK_CAT = 256
V_ROWS = V_HEAD + 16
MAIN_COLS = 4096
CONV_HALO = 32

TM_IN = 1024
TR_IN = 512
TN_IN = 512
TS_QKV = 512
TS_CONV = 512
L_HGRN = 128
HGRN_STEP = 512
TM_OUT = 512
TR_OUT = 256
TM_UP = 2048
TR_UP = 512
TF_UP = 512
TM_DOWN = 512
TN_DOWN = 512
MIB = 1024 * 1024


def _params(sem, vmem_mib):
    return pltpu.CompilerParams(dimension_semantics=sem, vmem_limit_bytes=vmem_mib * MIB)


def _rms_rows(x, w):
    ms = jnp.mean(x * x, axis=-1, keepdims=True)
    return x * lax.rsqrt(ms + RMS_EPS) * w


def _sigmoid(x):
    return 1.0 / (1.0 + jnp.exp(-x))


def _in_proj_body(x_ref, nw_ref, wm_ref, wk_ref, om_ref, ok_ref):
    for r in range(0, TM_IN, TR_IN):
        xn = _rms_rows(x_ref[r:r + TR_IN, :], nw_ref[...]).astype(BF16)
        for c in range(0, MAIN_COLS, TN_IN):
            om_ref[r:r + TR_IN, c:c + TN_IN] = jnp.dot(
                xn, wm_ref[:, c:c + TN_IN], preferred_element_type=F32).astype(BF16)
        ok_ref[r:r + TR_IN, :] = jnp.dot(xn, wk_ref[...], preferred_element_type=F32)


def _in_proj(h, norm_w, w_main, w_kr, l):
    t, d = h.shape
    once = pl.Buffered(1)
    return pl.pallas_call(
        _in_proj_body,
        grid=(t // TM_IN,),
        in_specs=[
            pl.BlockSpec((TM_IN, d), lambda i: (i, 0)),
            pl.BlockSpec((None, 1, d), lambda i: (l, 0, 0)),
            pl.BlockSpec((None, d, MAIN_COLS), lambda i: (l, 0, 0), pipeline_mode=once),
            pl.BlockSpec((None, d, 128), lambda i: (l, 0, 0), pipeline_mode=once),
        ],
        out_specs=[
            pl.BlockSpec((TM_IN, MAIN_COLS), lambda i: (i, 0)),
            pl.BlockSpec((TM_IN, 128), lambda i: (i, 0)),
        ],
        out_shape=[jax.ShapeDtypeStruct((t, MAIN_COLS), BF16), jax.ShapeDtypeStruct((t, 128), F32)],
        compiler_params=_params(("parallel",), 58),
        name="in_proj",
    )(h, norm_w, w_main, w_kr)


_NT = (((1,), (1,)), ((), ()))


def _q_proj_body(scale, c_ref, nw_ref, w_ref, cs_ref, o_ref):
    cn = _rms_rows(c_ref[...].astype(F32), nw_ref[...]).astype(BF16)
    cs = cs_ref[...] * scale
    for hd in range(MLA_HEADS):
        qt = lax.dot_general(w_ref[hd * QK_CAT:(hd + 1) * QK_CAT, :], cn, _NT,
                             preferred_element_type=F32)
        o_ref[hd, 0:QK_NOPE, :] = (qt[0:QK_NOPE, :] * scale).astype(BF16)
        o_ref[hd, QK_NOPE:QK_CAT, :] = (qt[QK_NOPE:QK_CAT, :] * cs).astype(BF16)


def _q_proj(proj_main, norm_w, w_uq_t, cs_t, l, b, s):
    tq = 2 * TS_QKV
    ns = s // tq
    scale = (QK_NOPE + QK_ROPE) ** -0.5 * math.log2(math.e)
    return pl.pallas_call(
        functools.partial(_q_proj_body, scale),
        grid=(b, ns),
        in_specs=[
            pl.BlockSpec((tq, Q_LORA), lambda bi, i: (bi * ns + i, 0)),
            pl.BlockSpec((None, 1, Q_LORA), lambda bi, i: (l, 0, 0)),
            pl.BlockSpec((None, MLA_HEADS * QK_CAT, Q_LORA), lambda bi, i: (l, 0, 0)),
            pl.BlockSpec((128, tq), lambda bi, i: (0, bi * ns + i)),
        ],
        out_specs=pl.BlockSpec((None, MLA_HEADS, None, QK_CAT, tq), lambda bi, i: (bi, 0, i, 0, 0)),
        out_shape=jax.ShapeDtypeStruct((b, MLA_HEADS, ns, QK_CAT, tq), BF16),
        compiler_params=_params(("parallel", "parallel"), 40),
        name="q_proj",
    )(proj_main, norm_w, w_uq_t, cs_t)


def _kv_proj_body(c_ref, kr_ref, nw_ref, wk_ref, wv_ref, cs_ref, k_ref, v_ref):
    cn = _rms_rows(c_ref[...].astype(F32), nw_ref[...]).astype(BF16)
    a = kr_ref[...] * cs_ref[...]
    krot = (a + pltpu.roll(a, 64, axis=1)).astype(BF16)
    ts = cn.shape[0]
    for h0 in range(0, MLA_HEADS, 2):
        kn = jnp.dot(cn, wk_ref[:, h0 * QK_NOPE:(h0 + 2) * QK_NOPE],
                     preferred_element_type=F32)
        vt = lax.dot_general(wv_ref[h0 * V_HEAD:(h0 + 2) * V_HEAD, :], cn, _NT,
                             preferred_element_type=F32)
        for d in range(2):
            hd = h0 + d
            k_ref[hd, :, 0:QK_NOPE] = kn[:, d * QK_NOPE:(d + 1) * QK_NOPE].astype(BF16)
            k_ref[hd, :, QK_NOPE:QK_CAT] = krot
            for kb in range(ts // TS_QKV):
                cols = slice(kb * TS_QKV, (kb + 1) * TS_QKV)
                v_ref[hd, kb, 0:V_HEAD, :] = vt[d * V_HEAD:(d + 1) * V_HEAD, cols].astype(BF16)
                v_ref[hd, kb, V_HEAD:V_ROWS, :] = jnp.ones((V_ROWS - V_HEAD, TS_QKV), BF16)


def _kv_proj(proj_main, kr, norm_w, w_k, w_v_t, cs, l, b, s):
    ts = 2 * TS_QKV
    ns = s // ts
    return pl.pallas_call(
        _kv_proj_body,
        grid=(b, ns),
        in_specs=[
            pl.BlockSpec((ts, KV_LORA), lambda bi, i: (bi * ns + i, 1)),
            pl.BlockSpec((ts, 128), lambda bi, i: (bi * ns + i, 0)),
            pl.BlockSpec((None, 1, KV_LORA), lambda bi, i: (l, 0, 0)),
            pl.BlockSpec((None, KV_LORA, MLA_HEADS * QK_NOPE), lambda bi, i: (l, 0, 0)),
            pl.BlockSpec((None, MLA_HEADS * V_HEAD, KV_LORA), lambda bi, i: (l, 0, 0)),
            pl.BlockSpec((ts, 128), lambda bi, i: (bi * ns + i, 0)),
        ],
        out_specs=[
            pl.BlockSpec((None, MLA_HEADS, ts, QK_CAT), lambda bi, i: (bi, 0, i, 0)),
            pl.BlockSpec((None, MLA_HEADS, ts // TS_QKV, V_ROWS, TS_QKV), lambda bi, i: (bi, 0, i, 0, 0)),
        ],
        out_shape=[
            jax.ShapeDtypeStruct((b, MLA_HEADS, s, QK_CAT), BF16),
            jax.ShapeDtypeStruct((b, MLA_HEADS, s // TS_QKV, V_ROWS, TS_QKV), BF16),
        ],
        compiler_params=_params(("parallel", "parallel"), 40),
        name="kv_proj",
    )(proj_main, kr, norm_w, w_k, w_v_t, cs)


def _attn_body(tk, nq, q_ref, k_ref, v_ref, nw_ref, o_ref,
               sa_ref, sb_ref, xa_ref, xb_ref, m_ref, acc_ref):
    tq = 2 * tk
    key = lax.broadcasted_iota(jnp.int32, (tk, tq), 0)
    qry = lax.broadcasted_iota(jnp.int32, (tk, tq), 1)
    causal = key <= qry
    causal_half = causal[:, 0:tk]

    for qi in range(nq):
        q0 = qi * tq

        def fill(s_ref, x_ref, j, lo=0, hi=tq, mask=None):
            start = j * tk if isinstance(j, int) else pl.multiple_of(j * tk, tk)
            sc = jnp.dot(k_ref[pl.ds(start, tk), :], q_ref[qi, :, lo:hi],
                         preferred_element_type=F32)
            if mask is not None:
                sc = jnp.where(mask, sc, -jnp.inf)
            s_ref[:, 0:hi - lo] = sc
            x_ref[:, 0:hi - lo] = jnp.max(sc, axis=0, keepdims=True)

        def update(s_ref, x_ref, j, lo=0, hi=tq):
            m = m_ref[:, lo:hi]
            m_new = jnp.maximum(m, x_ref[:, 0:hi - lo])
            p = jnp.exp2(s_ref[:, 0:hi - lo] - m_new)
            alpha = jnp.exp2(m - m_new)
            acc_ref[:, lo:hi] = alpha * acc_ref[:, lo:hi] + jnp.dot(
                v_ref[j], p.astype(BF16), preferred_element_type=F32)
            m_ref[:, lo:hi] = m_new

        m_ref[...] = jnp.full(m_ref.shape, -jnp.inf, F32)
        acc_ref[...] = jnp.zeros(acc_ref.shape, F32)
        first_mask = causal if qi == 0 else None
        fill(sa_ref, xa_ref, 0, mask=first_mask)

        def pair(jj):
            j0 = 2 * jj
            fill(sb_ref, xb_ref, j0 + 1)
            update(sa_ref, xa_ref, j0)
            fill(sa_ref, xa_ref, j0 + 2)
            update(sb_ref, xb_ref, j0 + 1)

        def two_pairs(it, carry):
            pair(2 * it)
            pair(2 * it + 1)
            return carry

        full = max(qi - 1, 0)
        if full >= 2:
            lax.fori_loop(0, full // 2, two_pairs, 0)
        if full % 2:
            pair(full - 1)
        if qi > 0:
            j0 = 2 * (qi - 1)
            fill(sb_ref, xb_ref, j0 + 1)
            update(sa_ref, xa_ref, j0)
            fill(sa_ref, xa_ref, j0 + 2, mask=causal)
            update(sb_ref, xb_ref, j0 + 1)
        fill(sb_ref, xb_ref, 2 * qi + 1, tk, tq, mask=causal_half)
        update(sa_ref, xa_ref, 2 * qi)
        update(sb_ref, xb_ref, 2 * qi + 1, tk, tq)
        o = (acc_ref[0:V_HEAD, :] / acc_ref[V_HEAD:V_HEAD + 1, :]).T
        o_ref[q0:q0 + tq, :] = _rms_rows(o, nw_ref[...]).astype(BF16)


def _attention(q_t, kcat, v_t, norm_w, l, b, s):
    tk = TS_QKV
    tq = 2 * tk
    nq = s // tq
    pitch = tq + 128
    return pl.pallas_call(
        functools.partial(_attn_body, tk, nq),
        grid=(b, MLA_HEADS),
        in_specs=[
            pl.BlockSpec((None, None, nq, QK_CAT, tq), lambda bi, hi: (bi, hi, 0, 0, 0)),
            pl.BlockSpec((None, None, s, QK_CAT), lambda bi, hi: (bi, hi, 0, 0)),
            pl.BlockSpec((None, None, s // tk, V_ROWS, tk), lambda bi, hi: (bi, hi, 0, 0, 0)),
            pl.BlockSpec((None, 1, V_HEAD), lambda bi, hi: (l, 0, 0)),
        ],
        out_specs=pl.BlockSpec((s, V_HEAD), lambda bi, hi: (bi, hi)),
        out_shape=jax.ShapeDtypeStruct((b * s, MLA_HEADS * V_HEAD), BF16),
        scratch_shapes=[
            pltpu.VMEM((tk, pitch), F32), pltpu.VMEM((tk, pitch), F32),
            pltpu.VMEM((1, tq), F32), pltpu.VMEM((1, tq), F32),
            pltpu.VMEM((1, tq), F32), pltpu.VMEM((V_ROWS, tq), F32),
        ],
        compiler_params=_params(("parallel", "parallel"), 48),
        name="mla_attention",
    )(q_t, kcat, v_t, norm_w)


def _conv_body(ts, u_ref, halo_ref, cw_ref, cb_ref, lw_ref, lb_ref, o_ref, hp_ref):
    i = pl.program_id(1)

    def glu(u):
        u = u.astype(F32)
        return u[:, 0:CONV_CH] * _sigmoid(u[:, CONV_CH:2 * CONV_CH])

    hp_ref[CONV_HALO:CONV_HALO + ts, :] = glu(u_ref[...])
    hp_ref[0:CONV_HALO, :] = jnp.where(i == 0, 0.0, glu(halo_ref[...]))
    off = CONV_HALO - (CONV_WIDTH - 1)
    acc = jnp.zeros((ts, CONV_CH), F32)
    hp = hp_ref[...]
    rows = hp.shape[0]
    for r in range(8):
        last = (CONV_WIDTH - 1 - r) // 8
        xr = pltpu.roll(hp, rows - (off + r), axis=0)
        for g in range(last + 1):
            acc = acc + xr[8 * g:8 * g + ts, :] * cw_ref[8 * g + r:8 * g + r + 1, :]
    acc = acc + cb_ref[...]
    mu = jnp.mean(acc, axis=-1, keepdims=True)
    xc = acc - mu
    var = jnp.mean(xc * xc, axis=-1, keepdims=True)
    y = xc * lax.rsqrt(var + LN_EPS) * lw_ref[...] + lb_ref[...]
    o_ref[...] = (y * _sigmoid(y)).astype(BF16)


def _conv(proj_main, conv_w, conv_b, ln_w, ln_b, l, b, s):
    ns = s // TS_CONV
    per = TS_CONV // CONV_HALO

    def halo_map(bi, i):
        return (jnp.maximum((bi * ns + i) * per - 1, 0), 1)

    return pl.pallas_call(
        functools.partial(_conv_body, TS_CONV),
        grid=(b, ns),
        in_specs=[
            pl.BlockSpec((TS_CONV, 2 * CONV_CH), lambda bi, i: (bi * ns + i, 1)),
            pl.BlockSpec((CONV_HALO, 2 * CONV_CH), halo_map),
            pl.BlockSpec((None, CONV_WIDTH, CONV_CH), lambda bi, i: (l, 0, 0)),
            pl.BlockSpec((None, 1, CONV_CH), lambda bi, i: (l, 0, 0)),
            pl.BlockSpec((None, 1, CONV_CH), lambda bi, i: (l, 0, 0)),
            pl.BlockSpec((None, 1, CONV_CH), lambda bi, i: (l, 0, 0)),
        ],
        out_specs=pl.BlockSpec((TS_CONV, CONV_CH), lambda bi, i: (bi * ns + i, 0)),
        out_shape=jax.ShapeDtypeStruct((b * s, CONV_CH), BF16),
        scratch_shapes=[pltpu.VMEM((CONV_HALO + TS_CONV, CONV_CH), F32)],
        compiler_params=_params(("parallel", "parallel"), 32),
        name="conv_module",
    )(proj_main, proj_main, conv_w, conv_b, ln_w, ln_b)


def _hgrn_tables(tile):
    levels = int(math.log2(tile))
    t = np.arange(tile)
    tri = (t[None, :] <= t[:, None]).astype(np.float32)
    mats = [tri]
    masks = [np.eye(tile, dtype=np.float32)]
    for j in range(levels):
        n = 1 << j
        t0 = (t & ~(2 * n - 1)) + n - 1
        diff = tri - (t[None, :] <= t0[:, None]).astype(np.float32)
        is_lower = (((t >> j) & 1) == 1)[:, None]
        mats.append(np.where(is_lower, diff, -diff))
        same_block = (t[:, None] >> (j + 1)) == (t[None, :] >> (j + 1))
        lower = ((t[:, None] >> j) & 1) == 1
        upper = ((t[None, :] >> j) & 1) == 0
        masks.append((same_block & lower & upper).astype(np.float32))
    stacked = np.concatenate(mats, axis=0)
    return levels, np.concatenate([stacked, stacked], axis=1), np.stack(masks, axis=0)


def _hgrn_body(tile, levels, hq_ref, hf_ref, hi_ref, hg_ref, lb_ref, nw_ref, mat_ref, mask_ref,
               o_ref, st_ref):
    i = pl.program_id(1)

    @pl.when(i == 0)
    def _():
        st_ref[...] = jnp.zeros_like(st_ref)

    for r0 in range(0, hq_ref.shape[0], tile):
        _hgrn_tile(tile, levels, slice(r0, r0 + tile), hq_ref, hf_ref, hi_ref, hg_ref, lb_ref, nw_ref,
                   mat_ref, mask_ref, o_ref, st_ref)


def _hgrn_tile(tile, levels, rows, hq_ref, hf_ref, hi_ref, hg_ref, lb_ref, nw_ref, mat_ref, mask_ref,
               o_ref, st_ref):
    heads = [slice(hd * HGRN_DK, (hd + 1) * HGRN_DK) for hd in range(HGRN_HEADS)]
    lb = lb_ref[...]
    z = hf_ref[rows, :].astype(F32)
    la = jnp.log(lb)
    lc = jnp.log1p(-lb) + (jnp.minimum(z, 0.0) - jnp.log1p(jnp.exp(-jnp.abs(z))))
    logf = jnp.maximum(la, lc) + jnp.log1p(jnp.exp(-jnp.abs(la - lc)))
    kk = (1.0 - lb) / (1.0 + jnp.exp(z))
    xq = hq_ref[rows, :].astype(F32)
    q = xq * _sigmoid(xq)
    xg = hg_ref[rows, :].astype(F32)
    g = xg * _sigmoid(xg)
    v = hi_ref[rows, :]

    lf2 = logf * math.log2(math.e)
    lf_hi = lf2.astype(BF16)
    lf_lo = (lf2 - lf_hi.astype(F32)).astype(BF16)
    dall = jnp.dot(mat_ref[...], jnp.concatenate([lf_hi, lf_lo], axis=0),
                   preferred_element_type=F32)
    bcum = dall[0:tile]
    blast = bcum[tile - 1:tile, :]
    q_b = q.astype(BF16)
    k_b = kk.astype(BF16)

    a = [mask_ref[0] * lax.dot_general(q_b[:, c], k_b[:, c], _NT, preferred_element_type=F32)
         for c in heads]
    for j in range(levels):
        wj = jnp.exp2(dall[(j + 1) * tile:(j + 2) * tile])
        qw = (q * wj).astype(BF16)
        kw = (kk * wj).astype(BF16)
        mj = mask_ref[j + 1]
        a = [a[hd] + mj * lax.dot_general(qw[:, c], kw[:, c], _NT, preferred_element_type=F32)
             for hd, c in enumerate(heads)]
    q_dec = (q * jnp.exp2(bcum)).astype(BF16)
    k_dec = (kk * jnp.exp2(blast - bcum)).astype(BF16)
    s_dec = jnp.exp2(blast)
    for hd, c in enumerate(heads):
        st = st_ref[hd]
        o = jnp.dot(a[hd].astype(BF16), v[:, c], preferred_element_type=F32)
        o = o + lax.dot_general(q_dec[:, c], st.astype(BF16), _NT, preferred_element_type=F32)
        st_ref[hd] = st * s_dec[:, c] + lax.dot_general(
            v[:, c], k_dec[:, c], (((0,), (0,)), ((), ())), preferred_element_type=F32)
        o_ref[rows, c] = (_rms_rows(o, nw_ref[...]) * g[:, c]).astype(BF16)


def _hgrn(proj_main, lb, norm_w, l, b, s):
    tile = L_HGRN
    step = HGRN_STEP
    nt = s // step
    levels, mats, masks = _hgrn_tables(tile)
    width = HGRN_HEADS * HGRN_DK
    row = lambda bi, i: bi * nt + i
    return pl.pallas_call(
        functools.partial(_hgrn_body, tile, levels),
        grid=(b, nt),
        in_specs=[
            pl.BlockSpec((step, width), lambda bi, i: (row(bi, i), 4)),
            pl.BlockSpec((step, width), lambda bi, i: (row(bi, i), 5)),
            pl.BlockSpec((step, width), lambda bi, i: (row(bi, i), 6)),
            pl.BlockSpec((step, width), lambda bi, i: (row(bi, i), 7)),
            pl.BlockSpec((None, 1, width), lambda bi, i: (l, 0, 0)),
            pl.BlockSpec((None, 1, HGRN_DV), lambda bi, i: (l, 0, 0)),
            pl.BlockSpec(((levels + 1) * tile, 2 * tile), lambda bi, i: (0, 0)),
            pl.BlockSpec((levels + 1, tile, tile), lambda bi, i: (0, 0, 0)),
        ],
        out_specs=pl.BlockSpec((step, width), lambda bi, i: (row(bi, i), 0)),
        out_shape=jax.ShapeDtypeStruct((b * s, width), BF16),
        scratch_shapes=[pltpu.VMEM((HGRN_HEADS, HGRN_DV, HGRN_DK), F32)],
        compiler_params=_params(("parallel", "arbitrary"), 32),
        name="hgrn2",
    )(proj_main, proj_main, proj_main, proj_main, lb, norm_w,
      jnp.asarray(mats, BF16), jnp.asarray(masks, F32))


def _out_proj_body(ya_ref, yb_ref, yc_ref, w_ref, h_ref, nw_ref, o_ref, xn_ref):
    na = ya_ref.shape[1]
    nb = na + yb_ref.shape[1]
    for r in range(0, TM_OUT, TR_OUT):
        rows = slice(r, r + TR_OUT)
        acc = jnp.dot(ya_ref[rows, :], w_ref[0:na, :], preferred_element_type=F32)
        acc = acc + jnp.dot(yb_ref[rows, :], w_ref[na:nb, :], preferred_element_type=F32)
        acc = acc + jnp.dot(yc_ref[rows, :], w_ref[nb:, :], preferred_element_type=F32)
        hn = h_ref[rows, :] + acc
        o_ref[rows, :] = hn
        xn_ref[rows, :] = _rms_rows(hn, nw_ref[...]).astype(BF16)


def _out_proj(ya, yb, yc, w_out, h, ffn_norm_w, l):
    t, d = h.shape
    dm = w_out.shape[1]
    return pl.pallas_call(
        _out_proj_body,
        grid=(t // TM_OUT,),
        in_specs=[
            pl.BlockSpec((TM_OUT, ya.shape[1]), lambda i: (i, 0)),
            pl.BlockSpec((TM_OUT, yb.shape[1]), lambda i: (i, 0)),
            pl.BlockSpec((TM_OUT, yc.shape[1]), lambda i: (i, 0)),
            pl.BlockSpec((None, dm, d), lambda i: (l, 0, 0), pipeline_mode=pl.Buffered(1)),
            pl.BlockSpec((TM_OUT, d), lambda i: (i, 0)),
            pl.BlockSpec((None, 1, d), lambda i: (l, 0, 0)),
        ],
        out_specs=[pl.BlockSpec((TM_OUT, d), lambda i: (i, 0)), pl.BlockSpec((TM_OUT, d), lambda i: (i, 0))],
        out_shape=[jax.ShapeDtypeStruct((t, d), F32), jax.ShapeDtypeStruct((t, d), BF16)],
        compiler_params=_params(("parallel",), 56),
        name="out_proj",
    )(ya, yb, yc, w_out, h, ffn_norm_w)


def _ffn_up_body(xn_ref, wg_ref, wu_ref, o_ref):
    for r in range(0, TM_UP, TR_UP):
        xn = xn_ref[r:r + TR_UP, :]
        for c in range(0, TF_UP, 256):
            gate = jnp.dot(xn, wg_ref[:, c:c + 256], preferred_element_type=F32)
            up = jnp.dot(xn, wu_ref[:, c:c + 256], preferred_element_type=F32)
            o_ref[r:r + TR_UP, c:c + 256] = (gate * _sigmoid(gate) * up).astype(BF16)


def _ffn_up(xn, w_gate, w_up, l):
    t, d = xn.shape
    f = w_gate.shape[2]
    return pl.pallas_call(
        _ffn_up_body,
        grid=(t // TM_UP, f // TF_UP),
        in_specs=[
            pl.BlockSpec((TM_UP, d), lambda i, j: (i, 0)),
            pl.BlockSpec((None, d, TF_UP), lambda i, j: (l, 0, j)),
            pl.BlockSpec((None, d, TF_UP), lambda i, j: (l, 0, j)),
        ],
        out_specs=pl.BlockSpec((TM_UP, TF_UP), lambda i, j: (i, j)),
        out_shape=jax.ShapeDtypeStruct((t, f), BF16),
        compiler_params=_params(("parallel", "arbitrary"), 40),
        name="ffn_up",
    )(xn, w_gate, w_up)


def _ffn_down_body(last, a_ref, w_ref, h_ref, fw_ref, o_ref):
    d = o_ref.shape[1]
    for c in range(0, d, TN_DOWN):
        o_ref[:, c:c + TN_DOWN] = h_ref[:, c:c + TN_DOWN] + jnp.dot(
            a_ref[...], w_ref[:, c:c + TN_DOWN], preferred_element_type=F32)
    if last:
        o_ref[...] = _rms_rows(o_ref[...], fw_ref[...])


def _ffn_down(ff, w_down, h, final_w, l, last):
    t, d = h.shape
    f = ff.shape[1]
    return pl.pallas_call(
        functools.partial(_ffn_down_body, last),
        grid=(t // TM_DOWN,),
        in_specs=[
            pl.BlockSpec((TM_DOWN, f), lambda i: (i, 0)),
            pl.BlockSpec((None, f, d), lambda i: (l, 0, 0), pipeline_mode=pl.Buffered(1)),
            pl.BlockSpec((TM_DOWN, d), lambda i: (i, 0)),
            pl.BlockSpec((1, d), lambda i: (0, 0)),
        ],
        out_specs=pl.BlockSpec((TM_DOWN, d), lambda i: (i, 0)),
        out_shape=jax.ShapeDtypeStruct((t, d), F32),
        compiler_params=_params(("parallel",), 58),
        name="ffn_down",
    )(ff, w_down, h, final_w)


def _rope_swap(cols):
    half = QK_ROPE // 2
    return np.concatenate([cols[half:], cols[:half]])


def _w_in_columns():
    kr0 = Q_LORA + KV_LORA
    main = np.concatenate([np.arange(0, kr0), np.arange(kr0 + QK_ROPE, kr0 + QK_ROPE + MAIN_COLS - kr0)])
    kr = np.arange(kr0, kr0 + QK_ROPE)
    return main, np.concatenate([kr, _rope_swap(kr)])


def _w_uq_columns():
    per = QK_NOPE + QK_ROPE
    out = []
    for hd in range(MLA_HEADS):
        pe = np.arange(hd * per + QK_NOPE, (hd + 1) * per)
        out += [np.arange(hd * per, hd * per + QK_NOPE), pe, _rope_swap(pe)]
    return np.concatenate(out)


def _w_ukv_columns():
    per = QK_NOPE + V_HEAD
    kn = [np.arange(hd * per, hd * per + QK_NOPE) for hd in range(MLA_HEADS)]
    vv = [np.arange(hd * per + QK_NOPE, (hd + 1) * per) for hd in range(MLA_HEADS)]
    return np.concatenate(kn), np.concatenate(vv)


@jax.jit
def _trunk(x, positions, attn_norm_w, w_in, q_norm_w, w_uq, kv_norm_w, w_ukv, mla_out_norm_w,
           conv_w, conv_b, conv_ln_w, conv_ln_b, hgrn_lower_bounds, hgrn_norm_w, w_out,
           ffn_norm_w, w_gate, w_up, w_down, final_norm_w):
    b, s, d = x.shape
    depth = w_in.shape[0]
    t = b * s
    for rows in (TM_IN, TM_OUT, TM_UP, TM_DOWN):
        assert t % rows == 0, (t, rows)
    for rows in (2 * TS_QKV, TS_CONV, HGRN_STEP):
        assert s % rows == 0, (s, rows)

    inv_freq = ROPE_THETA ** (-jnp.arange(0, QK_ROPE, 2, dtype=F32) / QK_ROPE)
    ang = positions.astype(F32)[..., None] * inv_freq
    cos, sin = jnp.cos(ang), jnp.sin(ang)
    cs = jnp.concatenate([cos, cos, -sin, sin], axis=-1).reshape(t, 2 * QK_ROPE)
    cs_t = cs.T
    lb_all = jnp.cumsum(jax.nn.softmax(hgrn_lower_bounds.astype(F32), axis=0), axis=0)
    lb_all = (lb_all - lb_all[0:1])[:, None, :]

    main_cols, kr_cols = _w_in_columns()
    kr0 = int(kr_cols[0])
    assert np.array_equal(main_cols, np.r_[0:kr0, kr0 + QK_ROPE:w_in.shape[2]])
    w_in_b = w_in.astype(BF16)
    w_main = jnp.concatenate([w_in_b[:, :, :kr0], w_in_b[:, :, kr0 + QK_ROPE:]], axis=2)
    w_kr = w_in_b[:, :, kr_cols]
    w_uq_t = jnp.swapaxes(w_uq[:, :, _w_uq_columns()], 1, 2).astype(BF16)
    kn_cols, v_cols = _w_ukv_columns()
    w_k_b = w_ukv[:, :, kn_cols].astype(BF16)
    w_v_t = jnp.swapaxes(w_ukv[:, :, v_cols], 1, 2).astype(BF16)
    w_out_b = w_out.astype(BF16)
    w_gate_b = w_gate.astype(BF16)
    w_up_b = w_up.astype(BF16)
    w_down_b = w_down.astype(BF16)
    row3 = lambda a: a[:, None, :]

    h = x.reshape(t, d)
    for l in range(depth):
        proj_main, kr = _in_proj(h, row3(attn_norm_w), w_main, w_kr, l)
        q_t = _q_proj(proj_main, row3(q_norm_w), w_uq_t, cs_t, l, b, s)
        kcat, v_t = _kv_proj(proj_main, kr, row3(kv_norm_w), w_k_b, w_v_t, cs, l, b, s)
        ya = _attention(q_t, kcat, v_t, row3(mla_out_norm_w), l, b, s)
        yb = _conv(proj_main, conv_w, row3(conv_b), row3(conv_ln_w), row3(conv_ln_b), l, b, s)
        yc = _hgrn(proj_main, lb_all, row3(hgrn_norm_w), l, b, s)
        h, xn = _out_proj(ya, yb, yc, w_out_b, h, row3(ffn_norm_w), l)
        ff = _ffn_up(xn, w_gate_b, w_up_b, l)
        h = _ffn_down(ff, w_down_b, h, final_norm_w[None, :], l, last=(l == depth - 1))
    return h.reshape(b, s, d)


def kernel(x, positions, attn_norm_w, w_in, q_norm_w, w_uq, kv_norm_w, w_ukv, mla_out_norm_w,
           conv_w, conv_b, conv_ln_w, conv_ln_b, hgrn_lower_bounds, hgrn_norm_w, w_out,
           ffn_norm_w, w_gate, w_up, w_down, final_norm_w):
    return _trunk(x, positions, attn_norm_w, w_in, q_norm_w, w_uq, kv_norm_w, w_ukv, mla_out_norm_w,
                  conv_w, conv_b, conv_ln_w, conv_ln_b, hgrn_lower_bounds, hgrn_norm_w, w_out,
                  ffn_norm_w, w_gate, w_up, w_down, final_norm_w)
```
